```python
import math
import jax, jax.numpy as jnp
from jax import lax
import numpy as np

D_MODEL = 1024
BATCH = 4
SEQ = 8192
DEPTH = 2
DEC_BATCH = 32
DEC_SEQ = 4
PAST_LEN = 16384
PAGE_SIZE = 128

N_AB = (DEPTH + 1) // 2
N_ML = DEPTH // 2

A_HEADS = 8
HEAD_DIM = 64
A_WIDTH = A_HEADS * HEAD_DIM
A_BRANCHES = ((128, 1), (512, 4), (2048, 16))
A_WIN = 2048
A_STEPS = A_BRANCHES[0][0] // A_BRANCHES[0][1]
A_BLK = 128
ATTN_SCALE = HEAD_DIM ** -0.5

POOL_SIZES = (2, 4, 8, 16)
N_POOL = len(POOL_SIZES)
B_WIDTH = D_MODEL // 2
B_GROUP = B_WIDTH // N_POOL
POOL_BUF = max(POOL_SIZES) - 1

AB_IN = 3 * A_WIDTH + B_WIDTH
AB_MIX = A_WIDTH + B_WIDTH

ML_INNER = 2 * D_MODEL
ML_HEADS = 4
ML_HD = ML_INNER // ML_HEADS
ML_CONV = 4
ML_QKV_BLOCK = 4
ML_CHUNK = 64

D_FF = 4 * D_MODEL

RMS_EPS = 1e-6
LN_EPS = 1e-5

kernel_name = 'hybrid_dilated_pool_mlstm_step'


def rms_norm(x, g):
    xf = x.astype(jnp.float32)
    y = xf * lax.rsqrt(jnp.mean(xf * xf, axis=-1, keepdims=True) + RMS_EPS)
    return (y * g.astype(jnp.float32)).astype(x.dtype)


def sq_relu_ffn(x, w1, w2):
    h = jax.nn.relu(x @ w1)
    return (h * h) @ w2


def softmax_stats(s):
    m = jnp.max(s, axis=-1, keepdims=True)
    p = jnp.exp(s - m)
    den = jnp.sum(p, axis=-1, keepdims=True)
    return p / den, (m + jnp.log(den))[..., 0]


def dilated_branch_prompt(q, k, v, dil):
    B, S, H, hd = q.shape
    L = S // dil
    nb = -(-L // A_BLK)
    Lp = nb * A_BLK

    def to_blocks(t):
        t = t.reshape(B, L, dil, H, hd).transpose(0, 2, 1, 3, 4)
        t = jnp.pad(t, ((0, 0), (0, 0), (0, Lp - L), (0, 0), (0, 0)))
        return t.reshape(B, dil, nb, A_BLK, H, hd)

    def band(t):
        prev = jnp.pad(t[:, :, :-1], ((0, 0), (0, 0), (1, 0), (0, 0), (0, 0), (0, 0)))
        return jnp.concatenate([prev, t], axis=3)

    qb = to_blocks(q)
    kk = band(to_blocks(k))
    vv = band(to_blocks(v))
    s = jnp.einsum('brnqhd,brnkhd->brnhqk', qb, kk) * ATTN_SCALE
    qi = jnp.arange(A_BLK)[:, None]
    ki = jnp.arange(2 * A_BLK)[None, :]
    dist = qi - ki + A_BLK
    key_step = jnp.arange(nb)[:, None, None] * A_BLK + ki[None] - A_BLK
    mask = ((dist >= 0) & (dist <= A_STEPS))[None] & (key_step >= 0)
    s = jnp.where(mask[:, None], s, -jnp.inf)
    p, lse = softmax_stats(s)
    o = jnp.einsum('brnhqk,brnkhd->brnqhd', p, vv)
    o = o.reshape(B, dil, Lp, H, hd)[:, :, :L].transpose(0, 2, 1, 3, 4).reshape(B, S, H, hd)
    lse = lse.transpose(0, 1, 2, 4, 3).reshape(B, dil, Lp, H)[:, :, :L].transpose(0, 2, 1, 3).reshape(B, S, H)
    return o, lse


def dilated_branch_sample(q, k_all, v_all, dil, buf_len):
    N, T, H, hd = q.shape
    nk = A_STEPS + 1
    idx = buf_len + jnp.arange(T)[:, None] - dil * jnp.arange(nk)[None, :]
    valid = idx >= 0
    flat = jnp.maximum(idx, 0).reshape(-1)
    kg = jnp.take(k_all, flat, axis=1).reshape(N, T, nk, H, hd)
    vg = jnp.take(v_all, flat, axis=1).reshape(N, T, nk, H, hd)
    s = jnp.einsum('nthd,ntkhd->nthk', q, kg) * ATTN_SCALE
    s = jnp.where(valid[None, :, None, :], s, -jnp.inf)
    p, lse = softmax_stats(s)
    o = jnp.einsum('nthk,ntkhd->nthd', p, vg)
    return o, lse


def combine_branches(outs, lses):
    w = jax.nn.softmax(jnp.stack(lses, 0), axis=0)
    return jnp.einsum('gnth,gnthd->nthd', w, jnp.stack(outs, 0))


def pool_mix(u, buf, start, w_pool, scale):
    N, T, _ = u.shape
    ext = jnp.concatenate([buf.astype(u.dtype), u], axis=1)
    cs = jnp.pad(jnp.cumsum(ext.astype(jnp.float32), axis=1), ((0, 0), (1, 0), (0, 0)))
    pos = (start + jnp.arange(T)).astype(jnp.float32)
    hi = cs[:, POOL_BUF + 1:POOL_BUF + 1 + T]
    means = []
    for g, w in enumerate(POOL_SIZES):
        sl = slice(g * B_GROUP, (g + 1) * B_GROUP)
        lo = cs[:, POOL_BUF + 1 - w:POOL_BUF + 1 - w + T, sl]
        cnt = jnp.minimum(pos + 1.0, float(w))
        means.append((hi[..., sl] - lo) / cnt[None, :, None])
    pooled = jnp.concatenate(means, axis=-1) - u.astype(jnp.float32)
    y = jnp.einsum('ntgc,gcd->ntgd', pooled.reshape(N, T, N_POOL, B_GROUP),
                   w_pool.astype(jnp.float32)).reshape(N, T, B_WIDTH)
    return (y * scale.astype(jnp.float32)).astype(u.dtype), ext[:, T:]


def ab_mix(xn, w_in, w_pool, pool_scale, w_out, kv_buf, pool_buf, start):
    N, T, _ = xn.shape
    proj = xn @ w_in
    q, k, v, u = jnp.split(proj, [A_WIDTH, 2 * A_WIDTH, 3 * A_WIDTH], axis=-1)
    heads = lambda t: t.reshape(N, T, A_HEADS, HEAD_DIM)
    k, v = heads(k), heads(v)
    qf, kf, vf = heads(q).astype(jnp.float32), k.astype(jnp.float32), v.astype(jnp.float32)
    if kv_buf is None:
        res = [dilated_branch_prompt(qf, kf, vf, d) for _, d in A_BRANCHES]
    else:
        k_buf, v_buf = kv_buf
        buf_len = k_buf.shape[1]
        k_all = jnp.concatenate([k_buf.astype(jnp.float32), kf], axis=1)
        v_all = jnp.concatenate([v_buf.astype(jnp.float32), vf], axis=1)
        res = [dilated_branch_sample(qf, k_all, v_all, d, buf_len) for _, d in A_BRANCHES]
    a = combine_branches([r[0] for r in res], [r[1] for r in res]).reshape(N, T, A_WIDTH).astype(xn.dtype)
    b, new_pool = pool_mix(u, pool_buf, start, w_pool, pool_scale)
    y = jnp.concatenate([a, b], axis=-1) @ w_out
    return y, k, v, new_pool


def headwise(x, w):
    N, T, C = x.shape
    xb = x.reshape(N, T, C // ML_QKV_BLOCK, ML_QKV_BLOCK)
    return jnp.einsum('ntgi,gio->ntgo', xb, w).reshape(N, T, C)


def mlstm_cell(q, k, v, i_pre, f_pre, C0, n0, m0):
    N, T, H, d = q.shape
    L = math.gcd(T, ML_CHUNK)
    nc = T // L

    def chunks(t):
        t = t.reshape((N, nc, L) + t.shape[2:])
        return jnp.moveaxis(t, (1, 3), (0, 2))

    xs = (chunks(q), chunks(k), chunks(v), chunks(i_pre), chunks(jax.nn.log_sigmoid(f_pre)))
    causal = jnp.tril(jnp.ones((L, L), dtype=bool))

    def step(carry, blk):
        C, n, m = carry
        qc, kc, vc, ic, fc = blk
        b = jnp.cumsum(fc, axis=-1)
        logw = jnp.where(causal, b[..., :, None] - b[..., None, :] + ic[..., None, :], -jnp.inf)
        inter = b + m[..., None]
        mt = jnp.maximum(inter, jnp.max(logw, axis=-1))
        a = jnp.exp(logw - mt[..., None]) * jnp.einsum('nhsd,nhrd->nhsr', qc, kc)
        si = jnp.exp(inter - mt)
        num = si[..., None] * jnp.einsum('nhsd,nhde->nhse', qc, C) + jnp.einsum('nhsr,nhre->nhse', a, vc)
        den = si * jnp.einsum('nhsd,nhd->nhs', qc, n) + jnp.sum(a, axis=-1)
        hc = num / jnp.maximum(jnp.abs(den), jnp.exp(-mt))[..., None]
        b_last = b[..., -1]
        wr = b_last[..., None] - b + ic
        m_new = jnp.maximum(b_last + m, jnp.max(wr, axis=-1))
        wk = jnp.exp(wr - m_new[..., None])[..., None] * kc
        sc = jnp.exp(b_last + m - m_new)
        C_new = sc[..., None, None] * C + jnp.einsum('nhrd,nhre->nhde', wk, vc)
        n_new = sc[..., None] * n + jnp.sum(wk, axis=2)
        return (C_new, n_new, m_new), hc

    (C, n, m), hs = lax.scan(step, (C0, n0, m0), xs)
    h = jnp.moveaxis(hs, (0, 2), (1, 3)).reshape(N, T, H, d)
    return h, C, n, m


def ml_mix(xn, w_in, w_conv, b_conv, w_q, w_k, w_v, w_i, b_i, w_f, b_f, g_norm, skip, w_out,
           conv_buf, C0, n0, m0):
    N, T, _ = xn.shape
    xm, og = jnp.split(xn @ w_in, 2, axis=-1)
    ext = jnp.concatenate([conv_buf.astype(xm.dtype), xm], axis=1)
    conv = sum((ext[:, j:j + T] * w_conv[j] for j in range(ML_CONV)), b_conv)
    ca = jax.nn.silu(conv)
    q = headwise(ca, w_q)
    k = headwise(ca, w_k)
    v = headwise(xm, w_v)
    qkv = jnp.concatenate([q, k, v], axis=-1)
    i_pre = (qkv @ w_i + b_i).astype(jnp.float32)
    f_pre = (qkv @ w_f + b_f).astype(jnp.float32)
    hd = lambda t: t.reshape(N, T, ML_HEADS, ML_HD).astype(jnp.float32)
    h, C, n, m = mlstm_cell(hd(q), hd(k) * (ML_HD ** -0.5), hd(v), i_pre, f_pre,
                            C0.astype(jnp.float32), n0.astype(jnp.float32), m0.astype(jnp.float32))
    mu = jnp.mean(h, axis=-1, keepdims=True)
    var = jnp.mean(jnp.square(h - mu), axis=-1, keepdims=True)
    hn = ((h - mu) * lax.rsqrt(var + LN_EPS)).reshape(N, T, ML_INNER) * g_norm.astype(jnp.float32)
    y = (hn + skip.astype(jnp.float32) * ca.astype(jnp.float32)) * jax.nn.sigmoid(og.astype(jnp.float32))
    return y.astype(xn.dtype) @ w_out, C, n, m, ext[:, T:]


def setup_inputs(seed: int = 0) -> dict:
    key = jax.random.key(seed)
    keys = iter(jax.random.split(key, 40))

    def rnd(shape, scale):
        return scale * jax.random.normal(next(keys), shape, jnp.float32)

    a_buf = min(A_WIN, PAST_LEN)
    qkv_shape = (N_ML, ML_INNER // ML_QKV_BLOCK, ML_QKV_BLOCK, ML_QKV_BLOCK)
    return {
        'x_prompt': rnd((BATCH, SEQ, D_MODEL), 1.0),
        'x_sample': rnd((DEC_BATCH, DEC_SEQ, D_MODEL), 1.0),
        'cache_a_k': rnd((N_AB, DEC_BATCH, a_buf, A_HEADS, HEAD_DIM), 1.0),
        'cache_a_v': rnd((N_AB, DEC_BATCH, a_buf, A_HEADS, HEAD_DIM), 1.0),
        'state_pool': rnd((N_AB, DEC_BATCH, POOL_BUF, B_WIDTH), 1.0),
        'state_ml_C': rnd((N_ML, DEC_BATCH, ML_HEADS, ML_HD, ML_HD), 0.1),
        'state_ml_n': rnd((N_ML, DEC_BATCH, ML_HEADS, ML_HD), 0.1),
        'state_ml_m': rnd((N_ML, DEC_BATCH, ML_HEADS), 0.5),
        'state_ml_conv': rnd((N_ML, DEC_BATCH, ML_CONV - 1, ML_INNER), 1.0),
        'norm_mix': 1.0 + rnd((DEPTH, D_MODEL), 0.05),
        'norm_ffn': 1.0 + rnd((DEPTH, D_MODEL), 0.05),
        'norm_final': 1.0 + rnd((D_MODEL,), 0.05),
        'ab_w_in': rnd((N_AB, D_MODEL, AB_IN), D_MODEL ** -0.5),
        'ab_w_pool': rnd((N_AB, N_POOL, B_GROUP, B_GROUP), B_GROUP ** -0.5),
        'ab_pool_scale': 1.0 + rnd((N_AB, B_WIDTH), 0.05),
        'ab_w_out': rnd((N_AB, AB_MIX, D_MODEL), AB_MIX ** -0.5),
        'ml_w_in': rnd((N_ML, D_MODEL, 2 * ML_INNER), D_MODEL ** -0.5),
        'ml_w_conv': rnd((N_ML, ML_CONV, ML_INNER), ML_CONV ** -0.5),
        'ml_b_conv': rnd((N_ML, ML_INNER), 0.02),
        'ml_w_q': rnd(qkv_shape, ML_QKV_BLOCK ** -0.5),
        'ml_w_k': rnd(qkv_shape, ML_QKV_BLOCK ** -0.5),
        'ml_w_v': rnd(qkv_shape, ML_QKV_BLOCK ** -0.5),
        'ml_w_i': rnd((N_ML, 3 * ML_INNER, ML_HEADS), 0.1 * (3 * ML_INNER) ** -0.5),
        'ml_b_i': rnd((N_ML, ML_HEADS), 0.1),
        'ml_w_f': rnd((N_ML, 3 * ML_INNER, ML_HEADS), 0.1 * (3 * ML_INNER) ** -0.5),
        'ml_b_f': jnp.linspace(3.0, 6.0, ML_HEADS, dtype=jnp.float32)[None] + rnd((N_ML, ML_HEADS), 0.01),
        'ml_norm': 1.0 + rnd((N_ML, ML_INNER), 0.05),
        'ml_skip': 1.0 + rnd((N_ML, ML_INNER), 0.05),
        'ml_w_out': rnd((N_ML, ML_INNER, D_MODEL), ML_INNER ** -0.5),
        'ffn_w1': rnd((DEPTH, D_MODEL, D_FF), D_MODEL ** -0.5),
        'ffn_w2': rnd((DEPTH, D_FF, D_MODEL), D_FF ** -0.5),
    }


def reference(x_prompt, x_sample, cache_a_k, cache_a_v, state_pool, state_ml_C, state_ml_n, state_ml_m,
              state_ml_conv, norm_mix, norm_ffn, norm_final, ab_w_in, ab_w_pool, ab_pool_scale, ab_w_out,
              ml_w_in, ml_w_conv, ml_b_conv, ml_w_q, ml_w_k, ml_w_v, ml_w_i, ml_b_i, ml_w_f, ml_b_f,
              ml_norm, ml_skip, ml_w_out, ffn_w1, ffn_w2):
    B, S, _ = x_prompt.shape
    NS = x_sample.shape[0]
    a_rows = min(A_WIN, S)
    hp, hs = x_prompt, x_sample
    p_ak, p_av, p_pool, p_C, p_n, p_m, p_conv = [], [], [], [], [], [], []
    s_ak, s_av, s_pool, s_C, s_n, s_m, s_conv = [], [], [], [], [], [], []
    for layer in range(DEPTH):
        j = layer // 2
        if layer % 2 == 0:
            w = (ab_w_in[j], ab_w_pool[j], ab_pool_scale[j], ab_w_out[j])
            y, k, v, pool = ab_mix(rms_norm(hp, norm_mix[layer]), *w, None,
                                   jnp.zeros((B, POOL_BUF, B_WIDTH), hp.dtype), 0)
            hp = hp + y
            p_ak.append(k[:, S - a_rows:])
            p_av.append(v[:, S - a_rows:])
            p_pool.append(pool)
            y, k, v, pool = ab_mix(rms_norm(hs, norm_mix[layer]), *w, (cache_a_k[j], cache_a_v[j]),
                                   state_pool[j], PAST_LEN)
            hs = hs + y
            s_ak.append(k)
            s_av.append(v)
            s_pool.append(pool)
        else:
            w = (ml_w_in[j], ml_w_conv[j], ml_b_conv[j], ml_w_q[j], ml_w_k[j], ml_w_v[j], ml_w_i[j], ml_b_i[j],
                 ml_w_f[j], ml_b_f[j], ml_norm[j], ml_skip[j], ml_w_out[j])
            y, C, n, m, conv = ml_mix(rms_norm(hp, norm_mix[layer]), *w,
                                      jnp.zeros((B, ML_CONV - 1, ML_INNER), hp.dtype),
                                      jnp.zeros((B, ML_HEADS, ML_HD, ML_HD), jnp.float32),
                                      jnp.zeros((B, ML_HEADS, ML_HD), jnp.float32),
                                      jnp.zeros((B, ML_HEADS), jnp.float32))
            hp = hp + y
            p_C.append(C)
            p_n.append(n)
            p_m.append(m)
            p_conv.append(conv)
            y, C, n, m, conv = ml_mix(rms_norm(hs, norm_mix[layer]), *w, state_ml_conv[j],
                                      state_ml_C[j], state_ml_n[j], state_ml_m[j])
            hs = hs + y
            s_C.append(C)
            s_n.append(n)
            s_m.append(m)
            s_conv.append(conv)
        hp = hp + sq_relu_ffn(rms_norm(hp, norm_ffn[layer]), ffn_w1[layer], ffn_w2[layer])
        hs = hs + sq_relu_ffn(rms_norm(hs, norm_ffn[layer]), ffn_w1[layer], ffn_w2[layer])
    y_prompt = rms_norm(hp, norm_final)
    y_sample = rms_norm(hs, norm_final)
    return (y_prompt, y_sample,
            jnp.stack(p_ak), jnp.stack(p_av), jnp.stack(p_pool),
            jnp.stack(p_C), jnp.stack(p_n), jnp.stack(p_m), jnp.stack(p_conv),
            jnp.stack(s_ak), jnp.stack(s_av), jnp.stack(s_pool),
            jnp.stack(s_C), jnp.stack(s_n), jnp.stack(s_m), jnp.stack(s_conv))
```

```python
import functools

import jax
import jax.numpy as jnp
from jax import lax
from jax.experimental import pallas as pl
from jax.experimental.pallas import tpu as pltpu

F32 = jnp.float32
BF16 = jnp.bfloat16

PAST_LEN = 16384
A_HEADS = 8
HEAD_DIM = 64
A_DILATIONS = (1, 4, 16)
A_STEPS = 128
A_BLK = 128
ATTN_SCALE = HEAD_DIM ** -0.5
POOL_SIZES = (2, 4, 8, 16)
POOL_HALO = 16
ML_HEADS = 4
ML_CONV = 4
ML_QKV_BLOCK = 4
ML_PROMPT_CHUNK = 256
ML_SAMPLE_PAD = 16
HEADWISE_CHUNK = 256
RMS_EPS = 1e-6
LN_EPS = 1e-5
NEG = -1e30
VMEM_LIMIT = 56 * 1024 * 1024


def _cparams(*sem):
    return pltpu.CompilerParams(dimension_semantics=sem, vmem_limit_bytes=VMEM_LIMIT)


def _const_spec(shape):
    nd = len(shape)
    return pl.BlockSpec(shape, lambda *_: (0,) * nd)


def _rms(x, g):
    return x * lax.rsqrt(jnp.mean(x * x, axis=-1, keepdims=True) + RMS_EPS) * g


def _log_sigmoid(x):
    return jnp.minimum(x, 0.0) - jnp.log(1.0 + jnp.exp(-jnp.abs(x)))


def _ab_in_kernel(x_ref, g_ref, w_ref, q_ref, kf_ref, vf_ref, kb_ref, vb_ref, u_ref):
    aw = q_ref.shape[-1]
    xn = _rms(x_ref[...], g_ref[...]).astype(BF16)
    p = jnp.dot(xn, w_ref[...], preferred_element_type=F32)
    q_ref[...] = (p[:, :aw] * ATTN_SCALE).astype(BF16)
    k = p[:, aw:2 * aw]
    v = p[:, 2 * aw:3 * aw]
    kf_ref[...] = k
    vf_ref[...] = v
    kb_ref[...] = k.astype(BF16)
    vb_ref[...] = v.astype(BF16)
    u_ref[...] = p[:, 3 * aw:]


def _ab_in(x, g, w, tm):
    m, d = x.shape
    aw = A_HEADS * HEAD_DIM
    bw = w.shape[1] - 3 * aw
    row = lambda n: pl.BlockSpec((tm, n), lambda i: (i, 0))
    return pl.pallas_call(
        _ab_in_kernel,
        grid=(m // tm,),
        in_specs=[row(d), _const_spec((1, d)), _const_spec(w.shape)],
        out_specs=[row(aw), row(aw), row(aw), row(aw), row(aw), row(bw)],
        out_shape=[jax.ShapeDtypeStruct((m, aw), BF16), jax.ShapeDtypeStruct((m, aw), F32),
                   jax.ShapeDtypeStruct((m, aw), F32), jax.ShapeDtypeStruct((m, aw), BF16),
                   jax.ShapeDtypeStruct((m, aw), BF16), jax.ShapeDtypeStruct((m, bw), F32)],
        compiler_params=_cparams("arbitrary"),
        name="ab_in",
    )(x, g, w)


def _attn_kernel(q_ref, kc_ref, kp_ref, vc_ref, vp_ref, o_ref, l_ref):
    n = pl.program_id(2)
    q = q_ref[0]
    k = jnp.concatenate([kp_ref[0], kc_ref[0]], axis=0)
    v = jnp.concatenate([vp_ref[0], vc_ref[0]], axis=0)
    qi = lax.broadcasted_iota(jnp.int32, (A_BLK, 2 * A_BLK), 0)
    ki = lax.broadcasted_iota(jnp.int32, (A_BLK, 2 * A_BLK), 1)
    dist = qi - ki + A_BLK
    first = jnp.where(n > 0, 0, A_BLK)
    mask = (dist >= 0) & (dist <= A_STEPS) & (ki >= first)
    lane = lax.broadcasted_iota(jnp.int32, (A_BLK, 2 * HEAD_DIM), 1)
    low = lane < HEAD_DIM
    for p in range(A_HEADS // 2):
        sl = slice(p * 2 * HEAD_DIM, (p + 1) * 2 * HEAD_DIM)
        qp = q[:, sl].astype(F32)
        kp = k[:, sl]
        vp = v[:, sl]
        outs, lses = [], []
        for j in range(2):
            qm = jnp.where(low if j == 0 else ~low, qp, 0.0).astype(BF16)
            s = lax.dot_general(qm, kp, (((1,), (1,)), ((), ())), preferred_element_type=F32)
            s = jnp.where(mask, s, NEG)
            m = jnp.max(s, axis=-1, keepdims=True)
            e = jnp.exp(s - m)
            den = jnp.sum(e, axis=-1, keepdims=True)
            outs.append(jnp.dot(e.astype(BF16), vp, preferred_element_type=F32) / den)
            lses.append(m + jnp.log(den))
        o_ref[0, :, sl] = jnp.where(low, outs[0], outs[1]).astype(o_ref.dtype)
        l_ref[0, :, sl] = jnp.where(low, lses[0], lses[1])


def _attn_branch(q, k, v, dil):
    b, s, aw = q.shape
    ln = s // dil
    nb = ln // A_BLK
    view = lambda t: t.reshape(b, ln, dil * aw)
    cur = pl.BlockSpec((1, A_BLK, aw), lambda bi, r, n: (bi, n, r))
    prev = pl.BlockSpec((1, A_BLK, aw), lambda bi, r, n: (bi, jnp.maximum(n - 1, 0), r))
    o, l = pl.pallas_call(
        _attn_kernel,
        grid=(b, dil, nb),
        in_specs=[cur, cur, prev, cur, prev],
        out_specs=[cur, cur],
        out_shape=[jax.ShapeDtypeStruct((b, ln, dil * aw), BF16), jax.ShapeDtypeStruct((b, ln, dil * aw), F32)],
        compiler_params=_cparams("arbitrary", "arbitrary", "arbitrary"),
        name=f"attn_d{dil}",
    )(view(q), view(k), view(k), view(v), view(v))
    return o.reshape(b * s, aw), l.reshape(b * s, aw)


def _sattn_kernel(q_ref, kn_ref, vn_ref, *refs, t_len):
    nd = len(A_DILATIONS)
    ck_refs, cv_refs, a_ref = refs[:nd], refs[nd:2 * nd], refs[2 * nd]
    aw = q_ref.shape[-1]
    hrow = lax.broadcasted_iota(jnp.int32, (A_HEADS, aw), 0)
    lane = lax.broadcasted_iota(jnp.int32, (A_HEADS, aw), 1)
    headmask = (lane // HEAD_DIM) == hrow
    kn = kn_ref[0].astype(BF16)
    vn = vn_ref[0].astype(BF16)
    nrows = kn.shape[0]
    ncol = lax.broadcasted_iota(jnp.int32, (A_HEADS, nrows), 1)
    ccol = lax.broadcasted_iota(jnp.int32, (A_HEADS, A_STEPS), 1)
    nt = (((1,), (1,)), ((), ()))
    rows = []
    for t in range(t_len):
        qt = q_ref[0, t:t + 1, :].astype(F32)
        qrows = jnp.where(headmask, jnp.broadcast_to(qt, (A_HEADS, aw)), 0.0).astype(BF16)
        s_new = lax.dot_general(qrows, kn, nt, preferred_element_type=F32)
        outs, lses = [], []
        for ck_ref, cv_ref, dil in zip(ck_refs, cv_refs, A_DILATIONS):
            s0 = t // dil + 1
            res = slice((t % dil) * aw, (t % dil + 1) * aw)
            kg = ck_ref[0, :, res].astype(BF16)
            vg = cv_ref[0, :, res].astype(BF16)
            s_c = lax.dot_general(qrows, kg, nt, preferred_element_type=F32)
            mask_c = ccol >= (s0 - 1)
            mask_n = functools.reduce(
                jnp.logical_or, [ncol == tp for tp in range(t + 1) if (t - tp) % dil == 0])
            s_c = jnp.where(mask_c, s_c, NEG)
            s_n = jnp.where(mask_n, s_new, NEG)
            m = jnp.maximum(jnp.max(s_c, axis=-1, keepdims=True), jnp.max(s_n, axis=-1, keepdims=True))
            p_c = jnp.exp(s_c - m)
            p_n = jnp.exp(s_n - m)
            den = jnp.sum(p_c, axis=-1, keepdims=True) + jnp.sum(p_n, axis=-1, keepdims=True)
            o = (jnp.dot(p_c.astype(BF16), vg, preferred_element_type=F32)
                 + jnp.dot(p_n.astype(BF16), vn, preferred_element_type=F32)) / den
            outs.append(o)
            lses.append(m + jnp.log(den))
        mm = functools.reduce(jnp.maximum, lses)
        es = [jnp.exp(l - mm) for l in lses]
        a8 = sum(e * o for e, o in zip(es, outs)) / sum(es)
        rows.append(jnp.sum(jnp.where(headmask, a8, 0.0), axis=0, keepdims=True))
    rows.append(jnp.zeros((a_ref.shape[1] - t_len, aw), F32))
    a_ref[0] = jnp.concatenate(rows, axis=0)


def _sattn(q, kn, vn, ck, cv, t_len):
    n, rows, aw = q.shape
    buf = ck.shape[1]
    seq = lambda r: pl.BlockSpec((1, r, aw), lambda i: (i, 0, 0))
    views, specs = [], []
    for dil in A_DILATIONS:
        assert buf % (A_STEPS * dil) == 0
        width = min(dil, t_len) * aw
        last = buf // (A_STEPS * dil) - 1
        views.append(lambda t, dil=dil: t.reshape(n, buf // dil, dil * aw))
        specs.append(pl.BlockSpec((1, A_STEPS, width), lambda i, last=last: (i, last, 0)))
    return pl.pallas_call(
        functools.partial(_sattn_kernel, t_len=t_len),
        grid=(n,),
        in_specs=[seq(rows), seq(kn.shape[1]), seq(kn.shape[1])] + specs + specs,
        out_specs=seq(rows),
        out_shape=jax.ShapeDtypeStruct((n, rows, aw), F32),
        compiler_params=_cparams("arbitrary"),
        name="sattn",
    )(q, kn, vn, *[f(ck) for f in views], *[f(cv) for f in views])


def _ab_out_kernel(*refs, g_seq, tiles_per_seq, start, branches, fresh):
    if branches:
        o_refs, l_refs, rest = refs[:3], refs[3:6], refs[6:]
    else:
        a_ref, rest = refs[0], refs[1:]
    u_ref, halo_ref, wp_ref, sc_ref, woa_ref, wob_ref, res_ref, h_ref, ext_ref = rest
    tm, bw = u_ref.shape
    hrows = POOL_HALO * g_seq
    i = pl.program_id(0)

    if branches:
        ls = [r[...] for r in l_refs]
        mm = jnp.maximum(jnp.maximum(ls[0], ls[1]), ls[2])
        es = [jnp.exp(l - mm) for l in ls]
        a = sum(e * r[...].astype(F32) for e, r in zip(es, o_refs)) / (es[0] + es[1] + es[2])
    else:
        a = a_ref[...]

    u = u_ref[...]
    halo = halo_ref[...]
    if fresh:
        halo = jnp.where(i % tiles_per_seq == 0, 0.0, halo)
    ext_ref[0:hrows, :] = halo
    ext_ref[hrows:hrows + tm, :] = u
    t_idx = lax.broadcasted_iota(jnp.int32, (tm, 1), 0) // g_seq
    pos = start + (i % tiles_per_seq) * (tm // g_seq) + t_idx
    gw = bw // len(POOL_SIZES)
    y = jnp.zeros((tm, woa_ref.shape[1]), F32)
    y += jnp.dot(a.astype(BF16), woa_ref[...], preferred_element_type=F32)
    for g, w in enumerate(POOL_SIZES):
        cols = slice(g * gw, (g + 1) * gw)
        acc = u[:, cols]
        for j in range(1, w):
            acc = acc + ext_ref[hrows - j * g_seq:hrows - j * g_seq + tm, cols]
        cnt = jnp.minimum(pos + 1, w).astype(F32)
        pooled = acc / cnt - u[:, cols]
        yg = jnp.dot(pooled.astype(BF16), wp_ref[g], preferred_element_type=F32) * sc_ref[:, cols]
        y += jnp.dot(yg.astype(BF16), wob_ref[cols, :], preferred_element_type=F32)
    h_ref[...] = res_ref[...] + y


def _ab_out(mix, u, halo, halo_spec, wp, scale, wo_a, wo_b, res, *, tm, g_seq, tiles_per_seq, start, fresh):
    m, bw = u.shape
    d = res.shape[1]
    row = lambda n: pl.BlockSpec((tm, n), lambda i: (i, 0))
    branches = len(mix) > 1
    kern = functools.partial(_ab_out_kernel, g_seq=g_seq, tiles_per_seq=tiles_per_seq, start=start,
                             branches=branches, fresh=fresh)
    return pl.pallas_call(
        kern,
        grid=(m // tm,),
        in_specs=[row(t.shape[1]) for t in mix] + [
            row(bw), halo_spec, _const_spec(wp.shape), _const_spec(scale.shape),
            _const_spec(wo_a.shape), _const_spec(wo_b.shape), row(d)],
        out_specs=row(d),
        out_shape=jax.ShapeDtypeStruct((m, d), F32),
        scratch_shapes=[pltpu.VMEM((POOL_HALO * g_seq + tm, bw), F32)],
        compiler_params=_cparams("arbitrary"),
        name="ab_out",
    )(*mix, u, halo, wp, scale, wo_a, wo_b, res)


def _ffn_kernel(*refs, final, fc):
    if final:
        x_ref, g_ref, w1_ref, w2_ref, gf_ref, o_ref = refs
    else:
        x_ref, g_ref, w1_ref, w2_ref, o_ref = refs
    x = x_ref[...]
    xn = _rms(x, g_ref[...]).astype(BF16)
    acc = jnp.zeros(x.shape, F32)
    for c in range(w1_ref.shape[1] // fc):
        h = jnp.maximum(jnp.dot(xn, w1_ref[:, c * fc:(c + 1) * fc], preferred_element_type=F32), 0.0)
        acc += jnp.dot((h * h).astype(BF16), w2_ref[c * fc:(c + 1) * fc, :], preferred_element_type=F32)
    out = x + acc
    if final:
        out = _rms(out, gf_ref[...])
    o_ref[...] = out


def _ffn(x, g, w1, w2, gf, tm):
    m, d = x.shape
    row = pl.BlockSpec((tm, d), lambda i: (i, 0))
    final = gf is not None
    args = [x, g, w1, w2] + ([gf] if final else [])
    specs = [row, _const_spec((1, d)), _const_spec(w1.shape), _const_spec(w2.shape)] + (
        [_const_spec((1, d))] if final else [])
    return pl.pallas_call(
        functools.partial(_ffn_kernel, final=final, fc=1024),
        grid=(m // tm,),
        in_specs=specs,
        out_specs=row,
        out_shape=jax.ShapeDtypeStruct((m, d), F32),
        compiler_params=_cparams("arbitrary"),
        name="ffn_final" if final else "ffn",
    )(*args)


def _ml_in_kernel(*refs, g_seq, tiles_per_seq, halo_steps, carry):
    if carry:
        (x_ref, g_ref, win_ref, wc_ref, bc_ref, wq_ref, wk_ref, wv_ref, wg_ref, bg_ref, skip_ref,
         q_ref, k_ref, v_ref, gate_ref, sg_ref, gsk_ref, tail_ref, ext_ref) = refs
    else:
        (x_ref, g_ref, win_ref, halo_ref, wc_ref, bc_ref, wq_ref, wk_ref, wv_ref, wg_ref, bg_ref, skip_ref,
         q_ref, k_ref, v_ref, gate_ref, sg_ref, gsk_ref, tail_ref, ext_ref) = refs
    tm = x_ref.shape[0]
    inner = q_ref.shape[1]
    hd = inner // ML_HEADS
    hrows = halo_steps * g_seq
    i = pl.program_id(0)

    xn = _rms(x_ref[...], g_ref[...]).astype(BF16)
    xm = jnp.dot(xn, win_ref[:, :inner], preferred_element_type=F32)
    og = jnp.dot(xn, win_ref[:, inner:], preferred_element_type=F32)

    if carry:
        @pl.when(i % tiles_per_seq == 0)
        def _():
            ext_ref[0:hrows, :] = jnp.zeros((hrows, inner), F32)
    else:
        ext_ref[0:hrows, :] = halo_ref[...]
    ext_ref[hrows:hrows + tm, :] = xm
    conv = xm * wc_ref[ML_CONV - 1:ML_CONV, :] + bc_ref[...]
    for j in range(ML_CONV - 1):
        off = hrows - (ML_CONV - 1 - j) * g_seq
        conv = conv + ext_ref[off:off + tm, :] * wc_ref[j:j + 1, :]
    tail = ext_ref[tm:tm + hrows, :]
    tail_ref[0] = tail
    if carry:
        ext_ref[0:hrows, :] = tail
    ca = conv * jax.nn.sigmoid(conv)
    sig = jax.nn.sigmoid(og)
    sg_ref[...] = sig.astype(BF16)
    gsk_ref[...] = (skip_ref[...] * ca * sig).astype(BF16)

    ca_b = ca.astype(BF16)
    xm_b = xm.astype(BF16)
    gates = jnp.zeros((tm, wg_ref.shape[1]), F32) + bg_ref[...]
    cw = HEADWISE_CHUNK
    for c in range(inner // cw):
        cols = slice(c * cw, (c + 1) * cw)
        qc = jnp.dot(ca_b[:, cols], wq_ref[c], preferred_element_type=F32)
        kc = jnp.dot(ca_b[:, cols], wk_ref[c], preferred_element_type=F32)
        vc = jnp.dot(xm_b[:, cols], wv_ref[c], preferred_element_type=F32)
        qb, kb, vb = qc.astype(BF16), kc.astype(BF16), vc.astype(BF16)
        q_ref[:, cols] = qb
        k_ref[:, cols] = (kc * hd ** -0.5).astype(BF16)
        v_ref[:, cols] = vb
        gates += jnp.dot(qb, wg_ref[c * cw:(c + 1) * cw, :], preferred_element_type=F32)
        gates += jnp.dot(kb, wg_ref[inner + c * cw:inner + (c + 1) * cw, :], preferred_element_type=F32)
        gates += jnp.dot(vb, wg_ref[2 * inner + c * cw:2 * inner + (c + 1) * cw, :], preferred_element_type=F32)
    gate_ref[...] = gates


def _ml_in(x, g, w_in, halo, wc, bc, wq, wk, wv, wg, bg, skip, *, tm, g_seq, tiles_per_seq, halo_steps):
    m, d = x.shape
    inner = wc.shape[1]
    carry = halo is None
    hrows = halo_steps * g_seq
    row = lambda n: pl.BlockSpec((tm, n), lambda i: (i, 0))
    consts = [wc, bc, wq, wk, wv, wg, bg, skip]
    args = [x, g, w_in] + ([] if carry else [halo]) + consts
    specs = ([row(d), _const_spec(g.shape), _const_spec(w_in.shape)]
             + ([] if carry else [_const_spec(halo.shape)]) + [_const_spec(t.shape) for t in consts])
    big = lambda dt: jax.ShapeDtypeStruct((m, inner), dt)
    return pl.pallas_call(
        functools.partial(_ml_in_kernel, g_seq=g_seq, tiles_per_seq=tiles_per_seq, halo_steps=halo_steps,
                          carry=carry),
        grid=(m // tm,),
        in_specs=specs,
        out_specs=[row(inner), row(inner), row(inner), row(wg.shape[1]), row(inner), row(inner),
                   pl.BlockSpec((1, hrows, inner), lambda i: (i, 0, 0))],
        out_shape=[big(BF16), big(BF16), big(BF16), jax.ShapeDtypeStruct((m, wg.shape[1]), F32),
                   big(BF16), big(BF16), jax.ShapeDtypeStruct((m // tm, hrows, inner), F32)],
        scratch_shapes=[pltpu.VMEM((hrows + tm, inner), F32)],
        compiler_params=_cparams("arbitrary"),
        name="ml_in",
    )(*args)


def _ml_cell_kernel(q_ref, k_ref, v_ref, gate_ref, sg_ref, gsk_ref, gn_ref, c0_ref, n0_ref, m0_ref,
                    y_ref, co_ref, no_ref, mo_ref, c_sc, n_sc, m_sc, *, t_valid):
    h = pl.program_id(1)
    c = pl.program_id(2)
    nc = pl.num_programs(2)
    ln = q_ref.shape[1]

    @pl.when(c == 0)
    def _():
        c_sc[...] = c0_ref[0, 0]
        n_sc[...] = n0_ref[0, 0]
        m_sc[...] = m0_ref[0, 0]

    q = q_ref[0]
    ks = k_ref[0]
    v = v_ref[0]
    gates = gate_ref[0]
    glane = lax.broadcasted_iota(jnp.int32, gates.shape, 1)
    i_col = jnp.sum(jnp.where(glane == h, gates, 0.0), axis=1, keepdims=True)
    f_col = jnp.sum(jnp.where(glane == h + ML_HEADS, gates, 0.0), axis=1, keepdims=True)
    fl_col = _log_sigmoid(f_col)
    if t_valid is not None:
        live = (c * ln + lax.broadcasted_iota(jnp.int32, (ln, 1), 0)) < t_valid
        i_col = jnp.where(live, i_col, NEG)
        fl_col = jnp.where(live, fl_col, 0.0)

    s_i = lax.broadcasted_iota(jnp.int32, (ln, ln), 0)
    r_i = lax.broadcasted_iota(jnp.int32, (ln, ln), 1)
    causal = r_i <= s_i
    eye = r_i == s_i
    to_row = lambda col: jnp.sum(jnp.where(eye, col, 0.0), axis=0, keepdims=True)
    fl_row = to_row(fl_col)
    i_row = to_row(i_col)
    b_col = jnp.sum(jnp.where(causal, fl_row, 0.0), axis=1, keepdims=True)
    b_row = to_row(b_col)
    m_old = m_sc[:, 0:1]

    logw = jnp.where(causal, b_col - b_row + i_row, NEG)
    inter = b_col + m_old
    mt = jnp.maximum(inter, jnp.max(logw, axis=1, keepdims=True))
    scores = lax.dot_general(q, ks, (((1,), (1,)), ((), ())), preferred_element_type=F32)
    a = jnp.exp(logw - mt) * scores
    si = jnp.exp(inter - mt)
    cmat = c_sc[...]
    nvec = n_sc[...]
    num = si * jnp.dot(q, cmat.astype(BF16), preferred_element_type=F32) \
        + jnp.dot(a.astype(BF16), v, preferred_element_type=F32)
    qn = jnp.sum(q.astype(F32) * nvec, axis=1, keepdims=True)
    den = si * qn + jnp.sum(a, axis=1, keepdims=True)
    hc = num / jnp.maximum(jnp.abs(den), jnp.exp(-mt))

    mu = jnp.mean(hc, axis=1, keepdims=True)
    dev = hc - mu
    var = jnp.mean(dev * dev, axis=1, keepdims=True)
    hn = dev * lax.rsqrt(var + LN_EPS) * gn_ref[...]
    y_ref[0] = (hn * sg_ref[0].astype(F32) + gsk_ref[0].astype(F32)).astype(y_ref.dtype)

    b_last = b_col[ln - 1:ln, :]
    wr = b_last - b_col + i_col
    m_new = jnp.maximum(b_last + m_old, jnp.max(wr, axis=0, keepdims=True))
    wk = jnp.exp(wr - m_new) * ks.astype(F32)
    sc = jnp.exp(b_last + m_old - m_new)
    upd = lax.dot_general(wk.astype(BF16), v, (((0,), (0,)), ((), ())), preferred_element_type=F32)
    c_new = sc * cmat + upd
    n_new = sc * nvec + jnp.sum(wk, axis=0, keepdims=True)
    c_sc[...] = c_new
    n_sc[...] = n_new
    m_sc[...] = jnp.broadcast_to(m_new, m_sc.shape)

    @pl.when(c == nc - 1)
    def _():
        co_ref[0, 0] = c_new
        no_ref[0, 0] = n_new
        mo_ref[0, 0] = jnp.broadcast_to(m_new, m_sc.shape)


def _ml_cell(q, k, v, gates, sg, gsk, gnorm, c0, n0, m0, *, chunk, t_valid):
    n, t, inner = q.shape
    hd = inner // ML_HEADS
    nc = t // chunk
    tok = pl.BlockSpec((1, chunk, hd), lambda b, h, c: (b, c, h))
    st = lambda r, w: pl.BlockSpec((1, 1, r, w), lambda b, h, c: (b, h, 0, 0))
    lanes = m0.shape[-1]
    return pl.pallas_call(
        functools.partial(_ml_cell_kernel, t_valid=t_valid),
        grid=(n, ML_HEADS, nc),
        in_specs=[tok, tok, tok, pl.BlockSpec((1, chunk, gates.shape[-1]), lambda b, h, c: (b, c, 0)),
                  tok, tok, pl.BlockSpec((1, hd), lambda b, h, c: (0, h)),
                  st(hd, hd), st(1, hd), st(1, lanes)],
        out_specs=[tok, st(hd, hd), st(1, hd), st(1, lanes)],
        out_shape=[jax.ShapeDtypeStruct((n, t, inner), BF16),
                   jax.ShapeDtypeStruct((n, ML_HEADS, hd, hd), F32),
                   jax.ShapeDtypeStruct((n, ML_HEADS, 1, hd), F32),
                   jax.ShapeDtypeStruct((n, ML_HEADS, 1, lanes), F32)],
        scratch_shapes=[pltpu.VMEM((hd, hd), F32), pltpu.VMEM((1, hd), F32), pltpu.VMEM((1, lanes), F32)],
        compiler_params=_cparams("arbitrary", "arbitrary", "arbitrary"),
        name="ml_cell",
    )(q, k, v, gates, sg, gsk, gnorm, c0, n0, m0)


def _proj_res_kernel(y_ref, w_ref, res_ref, o_ref):
    o_ref[...] = res_ref[...] + jnp.dot(y_ref[...], w_ref[...], preferred_element_type=F32)


def _proj_res(y, w, res, tm):
    m, kdim = y.shape
    d = res.shape[1]
    return pl.pallas_call(
        _proj_res_kernel,
        grid=(m // tm,),
        in_specs=[pl.BlockSpec((tm, kdim), lambda i: (i, 0)), _const_spec(w.shape),
                  pl.BlockSpec((tm, d), lambda i: (i, 0))],
        out_specs=pl.BlockSpec((tm, d), lambda i: (i, 0)),
        out_shape=jax.ShapeDtypeStruct((m, d), F32),
        compiler_params=_cparams("arbitrary"),
        name="ml_out",
    )(y, w, res)


def _headwise_dense(w):
    per = HEADWISE_CHUNK // ML_QKV_BLOCK
    w4 = w.reshape(-1, per, ML_QKV_BLOCK, ML_QKV_BLOCK)
    eye = jnp.eye(per, dtype=w.dtype)
    dense = w4[:, :, :, None, :] * eye[None, :, None, :, None]
    return dense.reshape(-1, HEADWISE_CHUNK, HEADWISE_CHUNK).astype(BF16)


def _time_major(t):
    return jnp.swapaxes(t, 0, 1).reshape(-1, t.shape[-1])


def _seq_major(t, n):
    return jnp.swapaxes(t.reshape(-1, n, t.shape[-1]), 0, 1)


def kernel(x_prompt, x_sample, cache_a_k, cache_a_v, state_pool, state_ml_C, state_ml_n, state_ml_m, state_ml_conv, norm_mix, norm_ffn, norm_final, ab_w_in, ab_w_pool, ab_pool_scale, ab_w_out, ml_w_in, ml_w_conv, ml_b_conv, ml_w_q, ml_w_k, ml_w_v, ml_w_i, ml_b_i, ml_w_f, ml_b_f, ml_norm, ml_skip, ml_w_out, ffn_w1, ffn_w2):
    b, s, d = x_prompt.shape
    ns, ts, _ = x_sample.shape
    aw = A_HEADS * HEAD_DIM
    inner = ml_w_conv.shape[-1]
    hd = inner // ML_HEADS
    tm = 512
    row1 = lambda t: t.reshape(1, -1)

    hp = x_prompt.reshape(b * s, d)
    hs = _time_major(x_sample)
    tms = hs.shape[0]

    w_in = ab_w_in[0].astype(BF16)
    wp = ab_w_pool[0].astype(BF16)
    psc = row1(ab_pool_scale[0])
    wo = ab_w_out[0].astype(BF16)
    wo_a, wo_b = wo[:aw], wo[aw:]
    g0 = row1(norm_mix[0])

    q, kf, vf, kb, vb, u = _ab_in(hp, g0, w_in, tm)
    to_seq = lambda t: t.reshape(b, s, aw)
    mix_o, mix_l = [], []
    for dil in A_DILATIONS:
        o, l = _attn_branch(to_seq(q), to_seq(kb), to_seq(vb), dil)
        mix_o.append(o)
        mix_l.append(l)
    tps = s // tm
    halo_spec = pl.BlockSpec((POOL_HALO, u.shape[1]), lambda i: (jnp.maximum(i * (tm // POOL_HALO) - 1, 0), 0))
    hp = _ab_out(mix_o + mix_l, u, u, halo_spec, wp, psc, wo_a, wo_b, hp,
                 tm=tm, g_seq=1, tiles_per_seq=tps, start=0, fresh=True)
    a_rows = min(A_STEPS * max(A_DILATIONS), s)
    heads = lambda t, n: t.reshape(n, -1, A_HEADS, HEAD_DIM)
    p_ak = heads(kf, b)[:, s - a_rows:][None]
    p_av = heads(vf, b)[:, s - a_rows:][None]
    pool_buf = state_pool.shape[2]
    p_pool = u.reshape(b, s, -1)[:, s - pool_buf:][None]

    qs, kfs, vfs, _, _, us = _ab_in(hs, g0, w_in, tms)
    s_ak = heads(_seq_major(kfs, ns), ns)[None]
    s_av = heads(_seq_major(vfs, ns), ns)[None]
    pad_t = lambda t, rows: jnp.pad(_seq_major(t, ns), ((0, 0), (0, rows - ts), (0, 0)))
    a_s = _sattn(pad_t(qs.astype(F32), 8), pad_t(kfs, 16), pad_t(vfs, 16),
                 cache_a_k[0].reshape(ns, -1, aw), cache_a_v[0].reshape(ns, -1, aw), ts)
    a_s = _time_major(a_s[:, :ts])
    halo_s = jnp.pad(_time_major(state_pool[0]), ((ns * (POOL_HALO - pool_buf), 0), (0, 0)))
    hs = _ab_out([a_s], us, halo_s, _const_spec(halo_s.shape), wp, psc, wo_a, wo_b, hs,
                 tm=tms, g_seq=ns, tiles_per_seq=1, start=PAST_LEN, fresh=False)
    s_pool = jnp.concatenate([state_pool[0], _seq_major(us, ns)], axis=1)[:, ts:][None]

    w1 = ffn_w1[0].astype(BF16)
    w2 = ffn_w2[0].astype(BF16)
    gf0 = row1(norm_ffn[0])
    hp = _ffn(hp, gf0, w1, w2, None, tm)
    hs = _ffn(hs, gf0, w1, w2, None, tms)

    g1 = row1(norm_mix[1])
    wi = ml_w_in[0].astype(BF16)
    wq, wk, wv = (_headwise_dense(t[0]) for t in (ml_w_q, ml_w_k, ml_w_v))
    glanes = 128
    wg = jnp.pad(jnp.concatenate([ml_w_i[0], ml_w_f[0]], axis=1), ((0, 0), (0, glanes - 2 * ML_HEADS))).astype(BF16)
    bg = jnp.pad(jnp.concatenate([ml_b_i[0], ml_b_f[0]]), (0, glanes - 2 * ML_HEADS)).reshape(1, glanes)
    consts = (ml_w_conv[0], row1(ml_b_conv[0]), wq, wk, wv, wg, bg, row1(ml_skip[0]))
    gn = row1(ml_norm[0])
    wout = ml_w_out[0].astype(BF16)
    conv_buf = ML_CONV - 1

    tm1 = 256
    halo_p = 8
    q1, k1, v1, gt, sg, gsk, tail = _ml_in(hp, g1, wi, None, *consts, tm=tm1, g_seq=1,
                                           tiles_per_seq=s // tm1, halo_steps=halo_p)
    seq3 = lambda t: t.reshape(b, s, -1)
    zeros_state = (jnp.zeros((b, ML_HEADS, hd, hd), F32), jnp.zeros((b, ML_HEADS, 1, hd), F32),
                   jnp.zeros((b, ML_HEADS, 1, 128), F32))
    y, p_c, p_n, p_m = _ml_cell(seq3(q1), seq3(k1), seq3(v1), seq3(gt), seq3(sg), seq3(gsk), gn, *zeros_state,
                                chunk=ML_PROMPT_CHUNK, t_valid=None)
    hp = _proj_res(y.reshape(b * s, inner), wout, hp, tm)
    p_conv = tail.reshape(b, s // tm1, halo_p, inner)[:, -1, halo_p - conv_buf:][None]

    halo_c = jnp.pad(_time_major(state_ml_conv[0]), ((ns, 0), (0, 0)))
    q1, k1, v1, gt, sg, gsk, tail = _ml_in(hs, g1, wi, halo_c, *consts, tm=tms, g_seq=ns,
                                           tiles_per_seq=1, halo_steps=conv_buf + 1)
    pad16 = lambda t: jnp.pad(_seq_major(t, ns), ((0, 0), (0, ML_SAMPLE_PAD - ts), (0, 0)))
    m0 = jnp.broadcast_to(state_ml_m[0][:, :, None, None], (ns, ML_HEADS, 1, 128))
    ys, s_c, s_n, s_m = _ml_cell(pad16(q1), pad16(k1), pad16(v1), pad16(gt), pad16(sg), pad16(gsk), gn,
                                 state_ml_C[0], state_ml_n[0][:, :, None, :], m0,
                                 chunk=ML_SAMPLE_PAD, t_valid=ts)
    hs = _proj_res(_time_major(ys[:, :ts]), wout, hs, tms)
    s_conv = _seq_major(tail[0], ns)[:, -conv_buf:][None]

    w1 = ffn_w1[1].astype(BF16)
    w2 = ffn_w2[1].astype(BF16)
    gf1 = row1(norm_ffn[1])
    gfin = row1(norm_final)
    y_prompt = _ffn(hp, gf1, w1, w2, gfin, tm).reshape(b, s, d)
    y_sample = _seq_major(_ffn(hs, gf1, w1, w2, gfin, tms), ns)

    return (y_prompt, y_sample, p_ak, p_av, p_pool,
            p_c[None], p_n[:, :, 0][None], p_m[:, :, 0, 0][None], p_conv,
            s_ak, s_av, s_pool,
            s_c[None], s_n[:, :, 0][None], s_m[:, :, 0, 0][None], s_conv)
```

```python
import functools

import jax
import jax.numpy as jnp
from jax import lax
from jax.experimental import pallas as pl
from jax.experimental.pallas import tpu as pltpu

F32 = jnp.float32
BF16 = jnp.bfloat16

PAST_LEN = 16384
A_HEADS = 8
HEAD_DIM = 64
A_DILATIONS = (1, 4, 16)
A_STEPS = 128
A_BLK = 128
A_SUPER = A_BLK * max(A_DILATIONS)
ATTN_SCALE = HEAD_DIM ** -0.5
POOL_SIZES = (2, 4, 8, 16)
POOL_HALO = 16
ML_HEADS = 4
ML_CONV = 4
ML_QKV_BLOCK = 4
ML_PROMPT_CHUNK = 256
ML_SAMPLE_PAD = 16
HEADWISE_CHUNK = 256
RMS_EPS = 1e-6
LN_EPS = 1e-5
NEG = -1e30
VMEM_LIMIT = 56 * 1024 * 1024


def _cparams(*sem):
    return pltpu.CompilerParams(dimension_semantics=sem, vmem_limit_bytes=VMEM_LIMIT)


def _const_spec(shape):
    nd = len(shape)
    return pl.BlockSpec(shape, lambda *_: (0,) * nd)


def _rms(x, g):
    return x * lax.rsqrt(jnp.mean(x * x, axis=-1, keepdims=True) + RMS_EPS) * g


def _log_sigmoid(x):
    return jnp.minimum(x, 0.0) - jnp.log(1.0 + jnp.exp(-jnp.abs(x)))


def _ab_in_kernel(x_ref, g_ref, w_ref, q_ref, kf_ref, vf_ref, kb_ref, vb_ref, u_ref):
    aw = q_ref.shape[-1]
    xn = _rms(x_ref[...], g_ref[...]).astype(BF16)
    p = jnp.dot(xn, w_ref[...], preferred_element_type=F32)
    q_ref[...] = (p[:, :aw] * ATTN_SCALE).astype(BF16)
    k = p[:, aw:2 * aw]
    v = p[:, 2 * aw:3 * aw]
    kf_ref[...] = k
    vf_ref[...] = v
    kb_ref[...] = k.astype(BF16)
    vb_ref[...] = v.astype(BF16)
    u_ref[...] = p[:, 3 * aw:]


def _ab_in(x, g, w, tm):
    m, d = x.shape
    aw = A_HEADS * HEAD_DIM
    bw = w.shape[1] - 3 * aw
    row = lambda n: pl.BlockSpec((tm, n), lambda i: (i, 0))
    return pl.pallas_call(
        _ab_in_kernel,
        grid=(m // tm,),
        in_specs=[row(d), _const_spec((1, d)), _const_spec(w.shape)],
        out_specs=[row(aw), row(aw), row(aw), row(aw), row(aw), row(bw)],
        out_shape=[jax.ShapeDtypeStruct((m, aw), BF16), jax.ShapeDtypeStruct((m, aw), F32),
                   jax.ShapeDtypeStruct((m, aw), F32), jax.ShapeDtypeStruct((m, aw), BF16),
                   jax.ShapeDtypeStruct((m, aw), BF16), jax.ShapeDtypeStruct((m, bw), F32)],
        compiler_params=_cparams("arbitrary"),
        name="ab_in",
    )(x, g, w)


def _attn_kernel(q_ref, kc_ref, kp_ref, vc_ref, vp_ref, a_ref, qs, ks, vs, os_, ls_):
    sb = pl.program_id(1)
    qs[...] = q_ref[0].astype(F32)
    ks[0:A_SUPER, :] = kp_ref[0].astype(F32)
    ks[A_SUPER:, :] = kc_ref[0].astype(F32)
    vs[0:A_SUPER, :] = vp_ref[0].astype(F32)
    vs[A_SUPER:, :] = vc_ref[0].astype(F32)
    qi = lax.broadcasted_iota(jnp.int32, (A_BLK, 2 * A_BLK), 0)
    ki = lax.broadcasted_iota(jnp.int32, (A_BLK, 2 * A_BLK), 1)
    dist = qi - ki + A_BLK
    band = (dist >= 0) & (dist <= A_STEPS)
    lane = lax.broadcasted_iota(jnp.int32, (A_BLK, 2 * HEAD_DIM), 1)
    low = lane < HEAD_DIM
    nt = (((1,), (1,)), ((), ()))

    for g, dil in enumerate(A_DILATIONS):
        blocks = A_SUPER // (A_BLK * dil)

        def body(idx, carry, g=g, dil=dil, blocks=blocks):
            r = idx // blocks
            n = idx % blocks
            q0 = n * (A_BLK * dil) + r
            k0 = A_SUPER + (n - 1) * (A_BLK * dil) + r
            if dil == 1:
                q0 = pl.multiple_of(q0, A_BLK)
                k0 = pl.multiple_of(k0, A_BLK)
                rows_q = pl.ds(q0, A_BLK)
                rows_k = pl.ds(k0, 2 * A_BLK)
            else:
                rows_q = pl.ds(q0, A_BLK, stride=dil)
                rows_k = pl.ds(k0, 2 * A_BLK, stride=dil)
            qp = qs[rows_q, :]
            kp = ks[rows_k, :].astype(BF16)
            vp = vs[rows_k, :].astype(BF16)
            first = jnp.where((n == 0) & (sb == 0), A_BLK, 0)
            mask = band & (ki >= first)
            outs, lses = [], []
            for j in range(2):
                qm = jnp.where(low if j == 0 else ~low, qp, 0.0).astype(BF16)
                s = lax.dot_general(qm, kp, nt, preferred_element_type=F32)
                s = jnp.where(mask, s, NEG)
                m = jnp.max(s, axis=-1, keepdims=True)
                e = jnp.exp(s - m)
                den = jnp.sum(e, axis=-1, keepdims=True)
                outs.append(jnp.dot(e.astype(BF16), vp, preferred_element_type=F32) / den)
                lses.append(m + jnp.log(den))
            os_[g, rows_q, :] = jnp.where(low, outs[0], outs[1])
            ls_[g, rows_q, :] = jnp.where(low, lses[0], lses[1])
            return carry

        lax.fori_loop(0, A_SUPER // A_BLK, body, 0)

    ls = [ls_[g] for g in range(len(A_DILATIONS))]
    mm = functools.reduce(jnp.maximum, ls)
    es = [jnp.exp(l - mm) for l in ls]
    num = sum(e * os_[g] for g, e in enumerate(es))
    a_ref[0] = (num / sum(es)).astype(a_ref.dtype)


def _attn(q, k, v):
    b, s, aw = q.shape
    pw = 2 * HEAD_DIM
    assert s % A_SUPER == 0
    cur = pl.BlockSpec((1, A_SUPER, pw), lambda bi, sb, p: (bi, sb, p))
    prev = pl.BlockSpec((1, A_SUPER, pw), lambda bi, sb, p: (bi, jnp.maximum(sb - 1, 0), p))
    nd = len(A_DILATIONS)
    return pl.pallas_call(
        _attn_kernel,
        grid=(b, s // A_SUPER, aw // pw),
        in_specs=[cur, cur, prev, cur, prev],
        out_specs=cur,
        out_shape=jax.ShapeDtypeStruct((b, s, aw), BF16),
        scratch_shapes=[pltpu.VMEM((A_SUPER, pw), F32), pltpu.VMEM((2 * A_SUPER, pw), F32),
                        pltpu.VMEM((2 * A_SUPER, pw), F32), pltpu.VMEM((nd, A_SUPER, pw), F32),
                        pltpu.VMEM((nd, A_SUPER, pw), F32)],
        compiler_params=_cparams("arbitrary", "arbitrary", "arbitrary"),
        name="attn",
    )(q, k, k, v, v)


def _sattn_kernel(q_ref, kn_ref, vn_ref, *refs, t_len):
    nd = len(A_DILATIONS)
    ck_refs, cv_refs, a_ref = refs[:nd], refs[nd:2 * nd], refs[2 * nd]
    rnd = lambda x: x.astype(BF16).astype(F32)
    kn = rnd(kn_ref[0])
    vn = rnd(vn_ref[0])
    for t in range(t_len):
        qt = q_ref[0, t]
        outs, lses = [], []
        for ck_ref, cv_ref, dil in zip(ck_refs, cv_refs, A_DILATIONS):
            j0 = t // dil
            kk = rnd(ck_ref[0, j0:, t % dil])
            vv = rnd(cv_ref[0, j0:, t % dil])
            news = [tp for tp in range(t + 1) if (t - tp) % dil == 0]
            s_c = jnp.sum(kk * qt[None], axis=-1, keepdims=True)
            s_n = [jnp.sum(kn[tp] * qt, axis=-1, keepdims=True) for tp in news]
            m = functools.reduce(jnp.maximum, s_n, jnp.max(s_c, axis=0))
            p_c = jnp.exp(s_c - m[None])
            p_n = [jnp.exp(x - m) for x in s_n]
            den = jnp.sum(p_c, axis=0) + sum(p_n)
            o = jnp.sum(rnd(p_c) * vv, axis=0) + sum(rnd(p) * vn[tp] for p, tp in zip(p_n, news))
            outs.append(o / den)
            lses.append(m + jnp.log(den))
        mm = functools.reduce(jnp.maximum, lses)
        es = [jnp.exp(l - mm) for l in lses]
        a_ref[0, t] = sum(e * o for e, o in zip(es, outs)) / sum(es)


def _sattn(q, kn, vn, ck, cv):
    n, t_len, nh, hd = q.shape
    buf = ck.shape[1]
    new = pl.BlockSpec((1, t_len, nh, hd), lambda i: (i, 0, 0, 0))
    views, specs = [], []
    for dil in A_DILATIONS:
        assert buf % (A_STEPS * dil) == 0
        last = buf // (A_STEPS * dil) - 1
        views.append(lambda t, dil=dil: t.reshape(n, buf // dil, dil, nh, hd))
        specs.append(pl.BlockSpec((1, A_STEPS, min(dil, t_len), nh, hd), lambda i, last=last: (i, last, 0, 0, 0)))
    return pl.pallas_call(
        functools.partial(_sattn_kernel, t_len=t_len),
        grid=(n,),
        in_specs=[new, new, new] + specs + specs,
        out_specs=new,
        out_shape=jax.ShapeDtypeStruct((n, t_len, nh, hd), F32),
        compiler_params=_cparams("arbitrary"),
        name="sattn",
    )(q, kn, vn, *[f(ck) for f in views], *[f(cv) for f in views])


def _ab_out_kernel(a_ref, u_ref, halo_ref, wp_ref, sc_ref, woa_ref, wob_ref, res_ref, h_ref, ext_ref, *,
                   g_seq, tiles_per_seq, start, fresh):
    tm, bw = u_ref.shape
    hrows = POOL_HALO * g_seq
    i = pl.program_id(0)
    a = a_ref[...]
    u = u_ref[...]
    halo = halo_ref[...]
    if fresh:
        halo = jnp.where(i % tiles_per_seq == 0, 0.0, halo)
    ext_ref[0:hrows, :] = halo
    ext_ref[hrows:hrows + tm, :] = u
    t_idx = lax.broadcasted_iota(jnp.int32, (tm, 1), 0) // g_seq
    pos = start + (i % tiles_per_seq) * (tm // g_seq) + t_idx
    gw = bw // len(POOL_SIZES)
    y = jnp.zeros((tm, woa_ref.shape[1]), F32)
    y += jnp.dot(a.astype(BF16), woa_ref[...], preferred_element_type=F32)
    for g, w in enumerate(POOL_SIZES):
        cols = slice(g * gw, (g + 1) * gw)
        acc = u[:, cols]
        for j in range(1, w):
            acc = acc + ext_ref[hrows - j * g_seq:hrows - j * g_seq + tm, cols]
        cnt = jnp.minimum(pos + 1, w).astype(F32)
        pooled = acc / cnt - u[:, cols]
        yg = jnp.dot(pooled.astype(BF16), wp_ref[g], preferred_element_type=F32) * sc_ref[:, cols]
        y += jnp.dot(yg.astype(BF16), wob_ref[cols, :], preferred_element_type=F32)
    h_ref[...] = res_ref[...] + y


def _ab_out(a, u, halo, halo_spec, wp, scale, wo_a, wo_b, res, *, tm, g_seq, tiles_per_seq, start, fresh):
    m, bw = u.shape
    d = res.shape[1]
    row = lambda n: pl.BlockSpec((tm, n), lambda i: (i, 0))
    kern = functools.partial(_ab_out_kernel, g_seq=g_seq, tiles_per_seq=tiles_per_seq, start=start, fresh=fresh)
    return pl.pallas_call(
        kern,
        grid=(m // tm,),
        in_specs=[row(a.shape[1]), row(bw), halo_spec, _const_spec(wp.shape), _const_spec(scale.shape),
                  _const_spec(wo_a.shape), _const_spec(wo_b.shape), row(d)],
        out_specs=row(d),
        out_shape=jax.ShapeDtypeStruct((m, d), F32),
        scratch_shapes=[pltpu.VMEM((POOL_HALO * g_seq + tm, bw), F32)],
        compiler_params=_cparams("arbitrary"),
        name="ab_out",
    )(a, u, halo, wp, scale, wo_a, wo_b, res)


def _ffn_kernel(*refs, final, fc):
    if final:
        x_ref, g_ref, w1_ref, w2_ref, gf_ref, o_ref = refs
    else:
        x_ref, g_ref, w1_ref, w2_ref, o_ref = refs
    x = x_ref[...]
    xn = _rms(x, g_ref[...]).astype(BF16)
    acc = jnp.zeros(x.shape, F32)
    for c in range(w1_ref.shape[1] // fc):
        h = jnp.maximum(jnp.dot(xn, w1_ref[:, c * fc:(c + 1) * fc], preferred_element_type=F32), 0.0)
        acc += jnp.dot((h * h).astype(BF16), w2_ref[c * fc:(c + 1) * fc, :], preferred_element_type=F32)
    out = x + acc
    if final:
        out = _rms(out, gf_ref[...])
    o_ref[...] = out


def _ffn(x, g, w1, w2, gf, tm):
    m, d = x.shape
    row = pl.BlockSpec((tm, d), lambda i: (i, 0))
    final = gf is not None
    args = [x, g, w1, w2] + ([gf] if final else [])
    specs = [row, _const_spec((1, d)), _const_spec(w1.shape), _const_spec(w2.shape)] + (
        [_const_spec((1, d))] if final else [])
    return pl.pallas_call(
        functools.partial(_ffn_kernel, final=final, fc=1024),
        grid=(m // tm,),
        in_specs=specs,
        out_specs=row,
        out_shape=jax.ShapeDtypeStruct((m, d), F32),
        compiler_params=_cparams("arbitrary"),
        name="ffn_final" if final else "ffn",
    )(*args)


def _ml_in_kernel(*refs, g_seq, tiles_per_seq, halo_steps, carry):
    if carry:
        (x_ref, g_ref, win_ref, wc_ref, bc_ref, wq_ref, wk_ref, wv_ref, wg_ref, bg_ref, skip_ref,
         q_ref, k_ref, v_ref, gate_ref, sg_ref, gsk_ref, tail_ref, ext_ref) = refs
    else:
        (x_ref, g_ref, win_ref, halo_ref, wc_ref, bc_ref, wq_ref, wk_ref, wv_ref, wg_ref, bg_ref, skip_ref,
         q_ref, k_ref, v_ref, gate_ref, sg_ref, gsk_ref, tail_ref, ext_ref) = refs
    tm = x_ref.shape[0]
    inner = q_ref.shape[1]
    hd = inner // ML_HEADS
    hrows = halo_steps * g_seq
    i = pl.program_id(0)

    xn = _rms(x_ref[...], g_ref[...]).astype(BF16)
    xm = jnp.dot(xn, win_ref[:, :inner], preferred_element_type=F32)
    og = jnp.dot(xn, win_ref[:, inner:], preferred_element_type=F32)

    if carry:
        @pl.when(i % tiles_per_seq == 0)
        def _():
            ext_ref[0:hrows, :] = jnp.zeros((hrows, inner), F32)
    else:
        ext_ref[0:hrows, :] = halo_ref[...]
    ext_ref[hrows:hrows + tm, :] = xm
    conv = xm * wc_ref[ML_CONV - 1:ML_CONV, :] + bc_ref[...]
    for j in range(ML_CONV - 1):
        off = hrows - (ML_CONV - 1 - j) * g_seq
        conv = conv + ext_ref[off:off + tm, :] * wc_ref[j:j + 1, :]
    tail = ext_ref[tm:tm + hrows, :]
    tail_ref[0] = tail
    if carry:
        ext_ref[0:hrows, :] = tail
    ca = conv * jax.nn.sigmoid(conv)
    sig = jax.nn.sigmoid(og)
    sg_ref[...] = sig.astype(BF16)
    gsk_ref[...] = (skip_ref[...] * ca * sig).astype(BF16)

    ca_b = ca.astype(BF16)
    xm_b = xm.astype(BF16)
    gates = jnp.zeros((tm, wg_ref.shape[1]), F32) + bg_ref[...]
    cw = HEADWISE_CHUNK
    for c in range(inner // cw):
        cols = slice(c * cw, (c + 1) * cw)
        qc = jnp.dot(ca_b[:, cols], wq_ref[c], preferred_element_type=F32)
        kc = jnp.dot(ca_b[:, cols], wk_ref[c], preferred_element_type=F32)
        vc = jnp.dot(xm_b[:, cols], wv_ref[c], preferred_element_type=F32)
        qb, kb, vb = qc.astype(BF16), kc.astype(BF16), vc.astype(BF16)
        q_ref[:, cols] = qb
        k_ref[:, cols] = (kc * hd ** -0.5).astype(BF16)
        v_ref[:, cols] = vb
        gates += jnp.dot(qb, wg_ref[c * cw:(c + 1) * cw, :], preferred_element_type=F32)
        gates += jnp.dot(kb, wg_ref[inner + c * cw:inner + (c + 1) * cw, :], preferred_element_type=F32)
        gates += jnp.dot(vb, wg_ref[2 * inner + c * cw:2 * inner + (c + 1) * cw, :], preferred_element_type=F32)
    gate_ref[...] = gates


def _ml_in(x, g, w_in, halo, wc, bc, wq, wk, wv, wg, bg, skip, *, tm, g_seq, tiles_per_seq, halo_steps):
    m, d = x.shape
    inner = wc.shape[1]
    carry = halo is None
    hrows = halo_steps * g_seq
    row = lambda n: pl.BlockSpec((tm, n), lambda i: (i, 0))
    consts = [wc, bc, wq, wk, wv, wg, bg, skip]
    args = [x, g, w_in] + ([] if carry else [halo]) + consts
    specs = ([row(d), _const_spec(g.shape), _const_spec(w_in.shape)]
             + ([] if carry else [_const_spec(halo.shape)]) + [_const_spec(t.shape) for t in consts])
    big = lambda dt: jax.ShapeDtypeStruct((m, inner), dt)
    return pl.pallas_call(
        functools.partial(_ml_in_kernel, g_seq=g_seq, tiles_per_seq=tiles_per_seq, halo_steps=halo_steps,
                          carry=carry),
        grid=(m // tm,),
        in_specs=specs,
        out_specs=[row(inner), row(inner), row(inner), row(wg.shape[1]), row(inner), row(inner),
                   pl.BlockSpec((1, hrows, inner), lambda i: (i, 0, 0))],
        out_shape=[big(BF16), big(BF16), big(BF16), jax.ShapeDtypeStruct((m, wg.shape[1]), F32),
                   big(BF16), big(BF16), jax.ShapeDtypeStruct((m // tm, hrows, inner), F32)],
        scratch_shapes=[pltpu.VMEM((hrows + tm, inner), F32)],
        compiler_params=_cparams("arbitrary"),
        name="ml_in",
    )(*args)


def _ml_cell_kernel(q_ref, k_ref, v_ref, gate_ref, sg_ref, gsk_ref, gn_ref, c0_ref, n0_ref, m0_ref,
                    y_ref, co_ref, no_ref, mo_ref, c_sc, n_sc, m_sc, *, t_valid):
    h = pl.program_id(1)
    c = pl.program_id(2)
    nc = pl.num_programs(2)
    ln = q_ref.shape[1]

    @pl.when(c == 0)
    def _():
        c_sc[...] = c0_ref[0, 0]
        n_sc[...] = n0_ref[0, 0]
        m_sc[...] = m0_ref[0, 0]

    q = q_ref[0]
    ks = k_ref[0]
    v = v_ref[0]
    gates = gate_ref[0]
    glane = lax.broadcasted_iota(jnp.int32, gates.shape, 1)
    i_col = jnp.sum(jnp.where(glane == h, gates, 0.0), axis=1, keepdims=True)
    f_col = jnp.sum(jnp.where(glane == h + ML_HEADS, gates, 0.0), axis=1, keepdims=True)
    fl_col = _log_sigmoid(f_col)
    if t_valid is not None:
        live = (c * ln + lax.broadcasted_iota(jnp.int32, (ln, 1), 0)) < t_valid
        i_col = jnp.where(live, i_col, NEG)
        fl_col = jnp.where(live, fl_col, 0.0)

    s_i = lax.broadcasted_iota(jnp.int32, (ln, ln), 0)
    r_i = lax.broadcasted_iota(jnp.int32, (ln, ln), 1)
    causal = r_i <= s_i
    eye = r_i == s_i
    to_row = lambda col: jnp.sum(jnp.where(eye, col, 0.0), axis=0, keepdims=True)
    fl_row = to_row(fl_col)
    i_row = to_row(i_col)
    b_col = jnp.sum(jnp.where(causal, fl_row, 0.0), axis=1, keepdims=True)
    b_row = to_row(b_col)
    m_old = m_sc[:, 0:1]

    logw = jnp.where(causal, b_col - b_row + i_row, NEG)
    inter = b_col + m_old
    mt = jnp.maximum(inter, jnp.max(logw, axis=1, keepdims=True))
    scores = lax.dot_general(q, ks, (((1,), (1,)), ((), ())), preferred_element_type=F32)
    a = jnp.exp(logw - mt) * scores
    si = jnp.exp(inter - mt)
    cmat = c_sc[...]
    nvec = n_sc[...]
    num = si * jnp.dot(q, cmat.astype(BF16), preferred_element_type=F32) \
        + jnp.dot(a.astype(BF16), v, preferred_element_type=F32)
    qn = jnp.sum(q.astype(F32) * nvec, axis=1, keepdims=True)
    den = si * qn + jnp.sum(a, axis=1, keepdims=True)
    hc = num / jnp.maximum(jnp.abs(den), jnp.exp(-mt))

    mu = jnp.mean(hc, axis=1, keepdims=True)
    dev = hc - mu
    var = jnp.mean(dev * dev, axis=1, keepdims=True)
    hn = dev * lax.rsqrt(var + LN_EPS) * gn_ref[...]
    y_ref[0] = (hn * sg_ref[0].astype(F32) + gsk_ref[0].astype(F32)).astype(y_ref.dtype)

    b_last = b_col[ln - 1:ln, :]
    wr = b_last - b_col + i_col
    m_new = jnp.maximum(b_last + m_old, jnp.max(wr, axis=0, keepdims=True))
    wk = jnp.exp(wr - m_new) * ks.astype(F32)
    sc = jnp.exp(b_last + m_old - m_new)
    upd = lax.dot_general(wk.astype(BF16), v, (((0,), (0,)), ((), ())), preferred_element_type=F32)
    c_new = sc * cmat + upd
    n_new = sc * nvec + jnp.sum(wk, axis=0, keepdims=True)
    c_sc[...] = c_new
    n_sc[...] = n_new
    m_sc[...] = jnp.broadcast_to(m_new, m_sc.shape)

    @pl.when(c == nc - 1)
    def _():
        co_ref[0, 0] = c_new
        no_ref[0, 0] = n_new
        mo_ref[0, 0] = jnp.broadcast_to(m_new, m_sc.shape)


def _ml_cell(q, k, v, gates, sg, gsk, gnorm, c0, n0, m0, *, chunk, t_valid):
    n, t, inner = q.shape
    hd = inner // ML_HEADS
    nc = t // chunk
    tok = pl.BlockSpec((1, chunk, hd), lambda b, h, c: (b, c, h))
    st = lambda r, w: pl.BlockSpec((1, 1, r, w), lambda b, h, c: (b, h, 0, 0))
    lanes = m0.shape[-1]
    return pl.pallas_call(
        functools.partial(_ml_cell_kernel, t_valid=t_valid),
        grid=(n, ML_HEADS, nc),
        in_specs=[tok, tok, tok, pl.BlockSpec((1, chunk, gates.shape[-1]), lambda b, h, c: (b, c, 0)),
                  tok, tok, pl.BlockSpec((1, hd), lambda b, h, c: (0, h)),
                  st(hd, hd), st(1, hd), st(1, lanes)],
        out_specs=[tok, st(hd, hd), st(1, hd), st(1, lanes)],
        out_shape=[jax.ShapeDtypeStruct((n, t, inner), BF16),
                   jax.ShapeDtypeStruct((n, ML_HEADS, hd, hd), F32),
                   jax.ShapeDtypeStruct((n, ML_HEADS, 1, hd), F32),
                   jax.ShapeDtypeStruct((n, ML_HEADS, 1, lanes), F32)],
        scratch_shapes=[pltpu.VMEM((hd, hd), F32), pltpu.VMEM((1, hd), F32), pltpu.VMEM((1, lanes), F32)],
        compiler_params=_cparams("arbitrary", "arbitrary", "arbitrary"),
        name="ml_cell",
    )(q, k, v, gates, sg, gsk, gnorm, c0, n0, m0)


def _proj_res_kernel(y_ref, w_ref, res_ref, o_ref):
    o_ref[...] = res_ref[...] + jnp.dot(y_ref[...], w_ref[...], preferred_element_type=F32)


def _proj_res(y, w, res, tm):
    m, kdim = y.shape
    d = res.shape[1]
    return pl.pallas_call(
        _proj_res_kernel,
        grid=(m // tm,),
        in_specs=[pl.BlockSpec((tm, kdim), lambda i: (i, 0)), _const_spec(w.shape),
                  pl.BlockSpec((tm, d), lambda i: (i, 0))],
        out_specs=pl.BlockSpec((tm, d), lambda i: (i, 0)),
        out_shape=jax.ShapeDtypeStruct((m, d), F32),
        compiler_params=_cparams("arbitrary"),
        name="ml_out",
    )(y, w, res)


def _headwise_dense(w):
    per = HEADWISE_CHUNK // ML_QKV_BLOCK
    w4 = w.reshape(-1, per, ML_QKV_BLOCK, ML_QKV_BLOCK)
    eye = jnp.eye(per, dtype=w.dtype)
    dense = w4[:, :, :, None, :] * eye[None, :, None, :, None]
    return dense.reshape(-1, HEADWISE_CHUNK, HEADWISE_CHUNK).astype(BF16)


def _time_major(t):
    return jnp.swapaxes(t, 0, 1).reshape(-1, t.shape[-1])


def _seq_major(t, n):
    return jnp.swapaxes(t.reshape(-1, n, t.shape[-1]), 0, 1)


def kernel(x_prompt, x_sample, cache_a_k, cache_a_v, state_pool, state_ml_C, state_ml_n, state_ml_m, state_ml_conv, norm_mix, norm_ffn, norm_final, ab_w_in, ab_w_pool, ab_pool_scale, ab_w_out, ml_w_in, ml_w_conv, ml_b_conv, ml_w_q, ml_w_k, ml_w_v, ml_w_i, ml_b_i, ml_w_f, ml_b_f, ml_norm, ml_skip, ml_w_out, ffn_w1, ffn_w2):
    b, s, d = x_prompt.shape
    ns, ts, _ = x_sample.shape
    aw = A_HEADS * HEAD_DIM
    inner = ml_w_conv.shape[-1]
    hd = inner // ML_HEADS
    tm = 512
    row1 = lambda t: t.reshape(1, -1)

    hp = x_prompt.reshape(b * s, d)
    hs = _time_major(x_sample)
    tms = hs.shape[0]

    w_in = ab_w_in[0].astype(BF16)
    wp = ab_w_pool[0].astype(BF16)
    psc = row1(ab_pool_scale[0])
    wo = ab_w_out[0].astype(BF16)
    wo_a, wo_b = wo[:aw], wo[aw:]
    g0 = row1(norm_mix[0])

    q, kf, vf, kb, vb, u = _ab_in(hp, g0, w_in, tm)
    to_seq = lambda t: t.reshape(b, s, aw)
    a_p = _attn(to_seq(q), to_seq(kb), to_seq(vb)).reshape(b * s, aw)
    tps = s // tm
    halo_spec = pl.BlockSpec((POOL_HALO, u.shape[1]), lambda i: (jnp.maximum(i * (tm // POOL_HALO) - 1, 0), 0))
    hp = _ab_out(a_p, u, u, halo_spec, wp, psc, wo_a, wo_b, hp,
                 tm=tm, g_seq=1, tiles_per_seq=tps, start=0, fresh=True)
    a_rows = min(A_STEPS * max(A_DILATIONS), s)
    heads = lambda t: t.reshape(t.shape[0], -1, A_HEADS, HEAD_DIM)
    p_ak = heads(to_seq(kf)[:, s - a_rows:])[None]
    p_av = heads(to_seq(vf)[:, s - a_rows:])[None]
    pool_buf = state_pool.shape[2]
    p_pool = u.reshape(b, s, -1)[:, s - pool_buf:][None]

    qs, kfs, vfs, _, _, us = _ab_in(hs, g0, w_in, tms)
    s_ak = heads(_seq_major(kfs, ns))
    s_av = heads(_seq_major(vfs, ns))
    a_s = _sattn(heads(_seq_major(qs.astype(F32), ns)), s_ak, s_av, cache_a_k[0], cache_a_v[0])
    s_ak, s_av = s_ak[None], s_av[None]
    a_s = _time_major(a_s.reshape(ns, ts, aw))
    halo_s = jnp.pad(_time_major(state_pool[0]), ((ns * (POOL_HALO - pool_buf), 0), (0, 0)))
    hs = _ab_out(a_s, us, halo_s, _const_spec(halo_s.shape), wp, psc, wo_a, wo_b, hs,
                 tm=tms, g_seq=ns, tiles_per_seq=1, start=PAST_LEN, fresh=False)
    s_pool = jnp.concatenate([state_pool[0], _seq_major(us, ns)], axis=1)[:, ts:][None]

    w1 = ffn_w1[0].astype(BF16)
    w2 = ffn_w2[0].astype(BF16)
    gf0 = row1(norm_ffn[0])
    hp = _ffn(hp, gf0, w1, w2, None, tm)
    hs = _ffn(hs, gf0, w1, w2, None, tms)

    g1 = row1(norm_mix[1])
    wi = ml_w_in[0].astype(BF16)
    wq, wk, wv = (_headwise_dense(t[0]) for t in (ml_w_q, ml_w_k, ml_w_v))
    glanes = 128
    wg = jnp.pad(jnp.concatenate([ml_w_i[0], ml_w_f[0]], axis=1), ((0, 0), (0, glanes - 2 * ML_HEADS))).astype(BF16)
    bg = jnp.pad(jnp.concatenate([ml_b_i[0], ml_b_f[0]]), (0, glanes - 2 * ML_HEADS)).reshape(1, glanes)
    consts = (ml_w_conv[0], row1(ml_b_conv[0]), wq, wk, wv, wg, bg, row1(ml_skip[0]))
    gn = row1(ml_norm[0])
    wout = ml_w_out[0].astype(BF16)
    conv_buf = ML_CONV - 1

    tm1 = 256
    halo_p = 8
    q1, k1, v1, gt, sg, gsk, tail = _ml_in(hp, g1, wi, None, *consts, tm=tm1, g_seq=1,
                                           tiles_per_seq=s // tm1, halo_steps=halo_p)
    seq3 = lambda t: t.reshape(b, s, -1)
    zeros_state = (jnp.zeros((b, ML_HEADS, hd, hd), F32), jnp.zeros((b, ML_HEADS, 1, hd), F32),
                   jnp.zeros((b, ML_HEADS, 1, 128), F32))
    y, p_c, p_n, p_m = _ml_cell(seq3(q1), seq3(k1), seq3(v1), seq3(gt), seq3(sg), seq3(gsk), gn, *zeros_state,
                                chunk=ML_PROMPT_CHUNK, t_valid=None)
    hp = _proj_res(y.reshape(b * s, inner), wout, hp, tm)
    p_conv = tail.reshape(b, s // tm1, halo_p, inner)[:, -1, halo_p - conv_buf:][None]

    halo_c = jnp.pad(_time_major(state_ml_conv[0]), ((ns, 0), (0, 0)))
    q1, k1, v1, gt, sg, gsk, tail = _ml_in(hs, g1, wi, halo_c, *consts, tm=tms, g_seq=ns,
                                           tiles_per_seq=1, halo_steps=conv_buf + 1)
    pad16 = lambda t: jnp.pad(_seq_major(t, ns), ((0, 0), (0, ML_SAMPLE_PAD - ts), (0, 0)))
    m0 = jnp.broadcast_to(state_ml_m[0][:, :, None, None], (ns, ML_HEADS, 1, 128))
    ys, s_c, s_n, s_m = _ml_cell(pad16(q1), pad16(k1), pad16(v1), pad16(gt), pad16(sg), pad16(gsk), gn,
                                 state_ml_C[0], state_ml_n[0][:, :, None, :], m0,
                                 chunk=ML_SAMPLE_PAD, t_valid=ts)
    hs = _proj_res(_time_major(ys[:, :ts]), wout, hs, tms)
    s_conv = _seq_major(tail[0], ns)[:, -conv_buf:][None]

    w1 = ffn_w1[1].astype(BF16)
    w2 = ffn_w2[1].astype(BF16)
    gf1 = row1(norm_ffn[1])
    gfin = row1(norm_final)
    y_prompt = _ffn(hp, gf1, w1, w2, gfin, tm).reshape(b, s, d)
    y_sample = _seq_major(_ffn(hs, gf1, w1, w2, gfin, tms), ns)

    return (y_prompt, y_sample, p_ak, p_av, p_pool,
            p_c[None], p_n[:, :, 0][None], p_m[:, :, 0, 0][None], p_conv,
            s_ak, s_av, s_pool,
            s_c[None], s_n[:, :, 0][None], s_m[:, :, 0, 0][None], s_conv)
```

```python
import functools

import jax
import jax.numpy as jnp
from jax import lax
from jax.experimental import pallas as pl
from jax.experimental.pallas import tpu as pltpu

F32 = jnp.float32
BF16 = jnp.bfloat16

PAST_LEN = 16384
A_HEADS = 8
HEAD_DIM = 64
A_DILATIONS = (1, 4, 16)
A_STEPS = 128
A_BLK = 128
A_SUPER = A_BLK * max(A_DILATIONS)
ATTN_SCALE = HEAD_DIM ** -0.5
POOL_SIZES = (2, 4, 8, 16)
POOL_HALO = 16
ML_HEADS = 4
ML_CONV = 4
ML_QKV_BLOCK = 4
ML_PROMPT_CHUNK = 256
ML_SAMPLE_PAD = 16
HEADWISE_CHUNK = 256
RMS_EPS = 1e-6
LN_EPS = 1e-5
NEG = -1e30
VMEM_LIMIT = 56 * 1024 * 1024


def _cparams(*sem):
    return pltpu.CompilerParams(dimension_semantics=sem, vmem_limit_bytes=VMEM_LIMIT)


def _const_spec(shape):
    nd = len(shape)
    return pl.BlockSpec(shape, lambda *_: (0,) * nd)


def _rms(x, g):
    return x * lax.rsqrt(jnp.mean(x * x, axis=-1, keepdims=True) + RMS_EPS) * g


def _log_sigmoid(x):
    return jnp.minimum(x, 0.0) - jnp.log(1.0 + jnp.exp(-jnp.abs(x)))


def _ab_in_kernel(x_ref, g_ref, w_ref, q_ref, kf_ref, vf_ref, kb_ref, vb_ref, u_ref):
    aw = q_ref.shape[-1]
    xn = _rms(x_ref[...], g_ref[...]).astype(BF16)
    p = jnp.dot(xn, w_ref[...], preferred_element_type=F32)
    q_ref[...] = (p[:, :aw] * ATTN_SCALE).astype(BF16)
    k = p[:, aw:2 * aw]
    v = p[:, 2 * aw:3 * aw]
    kf_ref[...] = k
    vf_ref[...] = v
    kb_ref[...] = k.astype(BF16)
    vb_ref[...] = v.astype(BF16)
    u_ref[...] = p[:, 3 * aw:]


def _ab_in(x, g, w, tm):
    m, d = x.shape
    aw = A_HEADS * HEAD_DIM
    bw = w.shape[1] - 3 * aw
    row = lambda n: pl.BlockSpec((tm, n), lambda i: (i, 0))
    return pl.pallas_call(
        _ab_in_kernel,
        grid=(m // tm,),
        in_specs=[row(d), _const_spec((1, d)), _const_spec(w.shape)],
        out_specs=[row(aw), row(aw), row(aw), row(aw), row(aw), row(bw)],
        out_shape=[jax.ShapeDtypeStruct((m, aw), BF16), jax.ShapeDtypeStruct((m, aw), F32),
                   jax.ShapeDtypeStruct((m, aw), F32), jax.ShapeDtypeStruct((m, aw), BF16),
                   jax.ShapeDtypeStruct((m, aw), BF16), jax.ShapeDtypeStruct((m, bw), F32)],
        compiler_params=_cparams("arbitrary"),
        name="ab_in",
    )(x, g, w)


def _attn_kernel(q_ref, kc_ref, kp_ref, vc_ref, vp_ref, a_ref, qs, ks, vs, os_, ls_, bias):
    sb = pl.program_id(1)
    qs[...] = q_ref[0].astype(F32)
    ks[0:A_SUPER, :] = kp_ref[0].astype(F32)
    ks[A_SUPER:, :] = kc_ref[0].astype(F32)
    vs[0:A_SUPER, :] = vp_ref[0].astype(F32)
    vs[A_SUPER:, :] = vc_ref[0].astype(F32)
    qi = lax.broadcasted_iota(jnp.int32, (A_BLK, 2 * A_BLK), 0)
    ki = lax.broadcasted_iota(jnp.int32, (A_BLK, 2 * A_BLK), 1)
    dist = qi - ki + A_BLK
    band = (dist >= 0) & (dist <= A_STEPS)
    bias[0] = jnp.where(band, 0.0, NEG)
    bias[1] = jnp.where(band & (ki >= A_BLK), 0.0, NEG)
    lane = lax.broadcasted_iota(jnp.int32, (A_BLK, 2 * HEAD_DIM), 1)
    low = lane < HEAD_DIM
    nt = (((1,), (1,)), ((), ()))

    for g, dil in enumerate(A_DILATIONS):
        blocks = A_SUPER // (A_BLK * dil)

        def body(idx, carry, g=g, dil=dil, blocks=blocks):
            r = idx // blocks
            n = idx % blocks
            q0 = n * (A_BLK * dil) + r
            k0 = A_SUPER + (n - 1) * (A_BLK * dil) + r
            if dil == 1:
                q0 = pl.multiple_of(q0, A_BLK)
                k0 = pl.multiple_of(k0, A_BLK)
                rows_q = pl.ds(q0, A_BLK)
                rows_k = pl.ds(k0, 2 * A_BLK)
            else:
                rows_q = pl.ds(q0, A_BLK, stride=dil)
                rows_k = pl.ds(k0, 2 * A_BLK, stride=dil)
            qp = qs[rows_q, :]
            kp = ks[rows_k, :].astype(BF16)
            vp = vs[rows_k, :].astype(BF16)
            mask_bias = bias[((n == 0) & (sb == 0)).astype(jnp.int32)]
            outs, lses = [], []
            for j in range(2):
                qm = jnp.where(low if j == 0 else ~low, qp, 0.0).astype(BF16)
                s = lax.dot_general(qm, kp, nt, preferred_element_type=F32) + mask_bias
                m = jnp.max(s, axis=-1, keepdims=True)
                e = jnp.exp(s - m)
                den = jnp.sum(e, axis=-1, keepdims=True)
                outs.append(jnp.dot(e.astype(BF16), vp, preferred_element_type=F32) / den)
                lses.append(m + jnp.log(den))
            os_[g, rows_q, :] = jnp.where(low, outs[0], outs[1])
            ls_[g, rows_q, :] = jnp.where(low, lses[0], lses[1])
            return carry

        lax.fori_loop(0, A_SUPER // A_BLK, body, 0, unroll=4)

    ls = [ls_[g] for g in range(len(A_DILATIONS))]
    mm = functools.reduce(jnp.maximum, ls)
    es = [jnp.exp(l - mm) for l in ls]
    num = sum(e * os_[g] for g, e in enumerate(es))
    a_ref[0] = (num / sum(es)).astype(a_ref.dtype)


def _attn(q, k, v):
    b, s, aw = q.shape
    pw = 2 * HEAD_DIM
    assert s % A_SUPER == 0
    cur = pl.BlockSpec((1, A_SUPER, pw), lambda bi, sb, p: (bi, sb, p))
    prev = pl.BlockSpec((1, A_SUPER, pw), lambda bi, sb, p: (bi, jnp.maximum(sb - 1, 0), p))
    nd = len(A_DILATIONS)
    return pl.pallas_call(
        _attn_kernel,
        grid=(b, s // A_SUPER, aw // pw),
        in_specs=[cur, cur, prev, cur, prev],
        out_specs=cur,
        out_shape=jax.ShapeDtypeStruct((b, s, aw), BF16),
        scratch_shapes=[pltpu.VMEM((A_SUPER, pw), F32), pltpu.VMEM((2 * A_SUPER, pw), F32),
                        pltpu.VMEM((2 * A_SUPER, pw), F32), pltpu.VMEM((nd, A_SUPER, pw), F32),
                        pltpu.VMEM((nd, A_SUPER, pw), F32), pltpu.VMEM((2, A_BLK, 2 * A_BLK), F32)],
        compiler_params=_cparams("arbitrary", "arbitrary", "arbitrary"),
        name="attn",
    )(q, k, k, v, v)


def _sattn_kernel(q_ref, kn_ref, vnt_ref, kt_ref, vt_ref, a_ref, *, t_len):
    nh, tp_rows, _ = q_ref.shape[1:]
    buf = kt_ref.shape[-1]
    rnd = lambda x: x.astype(BF16).astype(F32)
    trow = lax.broadcasted_iota(jnp.int32, (tp_rows, 1), 0)
    biases = []
    for dil in A_DILATIONS:
        span = A_STEPS * dil
        delta = (span + lax.broadcasted_iota(jnp.int32, (tp_rows, span), 0)
                 - lax.broadcasted_iota(jnp.int32, (tp_rows, span), 1))
        biases.append(jnp.where((delta % dil == 0) & (delta <= span), 0.0, NEG))
    for h in range(nh):
        qh = q_ref[0, h]
        knh = rnd(kn_ref[0, h])
        vnt = rnd(vnt_ref[0, h])
        vt = rnd(vt_ref[0, h])
        s_all = jnp.dot(qh.astype(BF16), kt_ref[0, h].astype(BF16), preferred_element_type=F32)
        s_new = [jnp.sum(qh * knh[tp:tp + 1, :], axis=-1, keepdims=True) for tp in range(t_len)]
        outs, lses = [], []
        for dil, bias in zip(A_DILATIONS, biases):
            lo = buf - A_STEPS * dil
            s = s_all[:, lo:] + bias
            s_n = [jnp.where((trow >= tp) & ((trow - tp) % dil == 0), x, NEG) for tp, x in enumerate(s_new)]
            m = functools.reduce(jnp.maximum, s_n, jnp.max(s, axis=-1, keepdims=True))
            p = jnp.exp(s - m)
            p_n = [jnp.exp(x - m) for x in s_n]
            den = jnp.sum(p, axis=-1, keepdims=True) + sum(p_n)
            pr = rnd(p)
            cols = []
            for t in range(t_len):
                acc = jnp.sum(pr[t:t + 1, :] * vt[:, lo:], axis=-1, keepdims=True)
                for tp in range(t + 1):
                    if (t - tp) % dil == 0:
                        acc = acc + rnd(p_n[tp][t:t + 1, :]) * vnt[:, tp:tp + 1]
                cols.append(acc / den[t:t + 1, :])
            outs.append(cols)
            lses.append(m + jnp.log(den))
        mm = functools.reduce(jnp.maximum, lses)
        es = [jnp.exp(l - mm) for l in lses]
        tot = sum(es)
        a_ref[0, h] = jnp.concatenate(
            [sum(e[t:t + 1, :] * o[t] for e, o in zip(es, outs)) / tot[t:t + 1, :] for t in range(t_len)], axis=1)


def _sattn(q, kn, vnt, kt, vt, t_len):
    n, nh, _, hd = q.shape
    buf = kt.shape[-1]
    assert buf >= A_STEPS * max(A_DILATIONS)
    spec = lambda t: pl.BlockSpec((1,) + t.shape[1:], lambda i: (i, 0, 0, 0))
    return pl.pallas_call(
        functools.partial(_sattn_kernel, t_len=t_len),
        grid=(n,),
        in_specs=[spec(t) for t in (q, kn, vnt, kt, vt)],
        out_specs=spec(vnt),
        out_shape=jax.ShapeDtypeStruct(vnt.shape, F32),
        compiler_params=_cparams("arbitrary"),
        name="sattn",
    )(q, kn, vnt, kt, vt)


def _ab_out_kernel(a_ref, u_ref, halo_ref, wp_ref, sc_ref, woa_ref, wob_ref, res_ref, h_ref, ext_ref, *,
                   g_seq, tiles_per_seq, start, fresh):
    tm, bw = u_ref.shape
    hrows = POOL_HALO * g_seq
    i = pl.program_id(0)
    a = a_ref[...]
    u = u_ref[...]
    halo = halo_ref[...]
    if fresh:
        halo = jnp.where(i % tiles_per_seq == 0, 0.0, halo)
    ext_ref[0:hrows, :] = halo
    ext_ref[hrows:hrows + tm, :] = u
    t_idx = lax.broadcasted_iota(jnp.int32, (tm, 1), 0) // g_seq
    pos = start + (i % tiles_per_seq) * (tm // g_seq) + t_idx
    gw = bw // len(POOL_SIZES)
    y = jnp.zeros((tm, woa_ref.shape[1]), F32)
    y += jnp.dot(a.astype(BF16), woa_ref[...], preferred_element_type=F32)
    for g, w in enumerate(POOL_SIZES):
        cols = slice(g * gw, (g + 1) * gw)
        acc = u[:, cols]
        for j in range(1, w):
            acc = acc + ext_ref[hrows - j * g_seq:hrows - j * g_seq + tm, cols]
        cnt = jnp.minimum(pos + 1, w).astype(F32)
        pooled = acc / cnt - u[:, cols]
        yg = jnp.dot(pooled.astype(BF16), wp_ref[g], preferred_element_type=F32) * sc_ref[:, cols]
        y += jnp.dot(yg.astype(BF16), wob_ref[cols, :], preferred_element_type=F32)
    h_ref[...] = res_ref[...] + y


def _ab_out(a, u, halo, halo_spec, wp, scale, wo_a, wo_b, res, *, tm, g_seq, tiles_per_seq, start, fresh):
    m, bw = u.shape
    d = res.shape[1]
    row = lambda n: pl.BlockSpec((tm, n), lambda i: (i, 0))
    kern = functools.partial(_ab_out_kernel, g_seq=g_seq, tiles_per_seq=tiles_per_seq, start=start, fresh=fresh)
    return pl.pallas_call(
        kern,
        grid=(m // tm,),
        in_specs=[row(a.shape[1]), row(bw), halo_spec, _const_spec(wp.shape), _const_spec(scale.shape),
                  _const_spec(wo_a.shape), _const_spec(wo_b.shape), row(d)],
        out_specs=row(d),
        out_shape=jax.ShapeDtypeStruct((m, d), F32),
        scratch_shapes=[pltpu.VMEM((POOL_HALO * g_seq + tm, bw), F32)],
        compiler_params=_cparams("arbitrary"),
        name="ab_out",
    )(a, u, halo, wp, scale, wo_a, wo_b, res)


def _ffn_kernel(*refs, final, fc):
    if final:
        x_ref, g_ref, w1_ref, w2_ref, gf_ref, o_ref = refs
    else:
        x_ref, g_ref, w1_ref, w2_ref, o_ref = refs
    x = x_ref[...]
    xn = _rms(x, g_ref[...]).astype(BF16)
    acc = jnp.zeros(x.shape, F32)
    for c in range(w1_ref.shape[1] // fc):
        h = jnp.maximum(jnp.dot(xn, w1_ref[:, c * fc:(c + 1) * fc], preferred_element_type=F32), 0.0)
        acc += jnp.dot((h * h).astype(BF16), w2_ref[c * fc:(c + 1) * fc, :], preferred_element_type=F32)
    out = x + acc
    if final:
        out = _rms(out, gf_ref[...])
    o_ref[...] = out


def _ffn(x, g, w1, w2, gf, tm):
    m, d = x.shape
    row = pl.BlockSpec((tm, d), lambda i: (i, 0))
    final = gf is not None
    args = [x, g, w1, w2] + ([gf] if final else [])
    specs = [row, _const_spec((1, d)), _const_spec(w1.shape), _const_spec(w2.shape)] + (
        [_const_spec((1, d))] if final else [])
    return pl.pallas_call(
        functools.partial(_ffn_kernel, final=final, fc=1024),
        grid=(m // tm,),
        in_specs=specs,
        out_specs=row,
        out_shape=jax.ShapeDtypeStruct((m, d), F32),
        compiler_params=_cparams("arbitrary"),
        name="ffn_final" if final else "ffn",
    )(*args)


def _ml_in_kernel(*refs, g_seq, tiles_per_seq, halo_steps, carry):
    if carry:
        (x_ref, g_ref, win_ref, wc_ref, bc_ref, wq_ref, wk_ref, wv_ref, wg_ref, bg_ref, skip_ref,
         q_ref, k_ref, v_ref, gate_ref, sg_ref, gsk_ref, tail_ref, ext_ref) = refs
    else:
        (x_ref, g_ref, win_ref, halo_ref, wc_ref, bc_ref, wq_ref, wk_ref, wv_ref, wg_ref, bg_ref, skip_ref,
         q_ref, k_ref, v_ref, gate_ref, sg_ref, gsk_ref, tail_ref, ext_ref) = refs
    tm = x_ref.shape[0]
    inner = q_ref.shape[1]
    hd = inner // ML_HEADS
    hrows = halo_steps * g_seq
    i = pl.program_id(0)

    xn = _rms(x_ref[...], g_ref[...]).astype(BF16)
    xm = jnp.dot(xn, win_ref[:, :inner], preferred_element_type=F32)
    og = jnp.dot(xn, win_ref[:, inner:], preferred_element_type=F32)

    if carry:
        @pl.when(i % tiles_per_seq == 0)
        def _():
            ext_ref[0:hrows, :] = jnp.zeros((hrows, inner), F32)
    else:
        ext_ref[0:hrows, :] = halo_ref[...]
    ext_ref[hrows:hrows + tm, :] = xm
    conv = xm * wc_ref[ML_CONV - 1:ML_CONV, :] + bc_ref[...]
    for j in range(ML_CONV - 1):
        off = hrows - (ML_CONV - 1 - j) * g_seq
        conv = conv + ext_ref[off:off + tm, :] * wc_ref[j:j + 1, :]
    tail = ext_ref[tm:tm + hrows, :]
    tail_ref[0] = tail
    if carry:
        ext_ref[0:hrows, :] = tail
    ca = conv * jax.nn.sigmoid(conv)
    sig = jax.nn.sigmoid(og)
    sg_ref[...] = sig.astype(BF16)
    gsk_ref[...] = (skip_ref[...] * ca * sig).astype(BF16)

    ca_b = ca.astype(BF16)
    xm_b = xm.astype(BF16)
    gates = jnp.zeros((tm, wg_ref.shape[1]), F32) + bg_ref[...]
    cw = HEADWISE_CHUNK
    for c in range(inner // cw):
        cols = slice(c * cw, (c + 1) * cw)
        qc = jnp.dot(ca_b[:, cols], wq_ref[c], preferred_element_type=F32)
        kc = jnp.dot(ca_b[:, cols], wk_ref[c], preferred_element_type=F32)
        vc = jnp.dot(xm_b[:, cols], wv_ref[c], preferred_element_type=F32)
        qb, kb, vb = qc.astype(BF16), kc.astype(BF16), vc.astype(BF16)
        q_ref[:, cols] = qb
        k_ref[:, cols] = (kc * hd ** -0.5).astype(BF16)
        v_ref[:, cols] = vb
        gates += jnp.dot(qb, wg_ref[c * cw:(c + 1) * cw, :], preferred_element_type=F32)
        gates += jnp.dot(kb, wg_ref[inner + c * cw:inner + (c + 1) * cw, :], preferred_element_type=F32)
        gates += jnp.dot(vb, wg_ref[2 * inner + c * cw:2 * inner + (c + 1) * cw, :], preferred_element_type=F32)
    gate_ref[...] = gates


def _ml_in(x, g, w_in, halo, wc, bc, wq, wk, wv, wg, bg, skip, *, tm, g_seq, tiles_per_seq, halo_steps):
    m, d = x.shape
    inner = wc.shape[1]
    carry = halo is None
    hrows = halo_steps * g_seq
    row = lambda n: pl.BlockSpec((tm, n), lambda i: (i, 0))
    consts = [wc, bc, wq, wk, wv, wg, bg, skip]
    args = [x, g, w_in] + ([] if carry else [halo]) + consts
    specs = ([row(d), _const_spec(g.shape), _const_spec(w_in.shape)]
             + ([] if carry else [_const_spec(halo.shape)]) + [_const_spec(t.shape) for t in consts])
    big = lambda dt: jax.ShapeDtypeStruct((m, inner), dt)
    return pl.pallas_call(
        functools.partial(_ml_in_kernel, g_seq=g_seq, tiles_per_seq=tiles_per_seq, halo_steps=halo_steps,
                          carry=carry),
        grid=(m // tm,),
        in_specs=specs,
        out_specs=[row(inner), row(inner), row(inner), row(wg.shape[1]), row(inner), row(inner),
                   pl.BlockSpec((1, hrows, inner), lambda i: (i, 0, 0))],
        out_shape=[big(BF16), big(BF16), big(BF16), jax.ShapeDtypeStruct((m, wg.shape[1]), F32),
                   big(BF16), big(BF16), jax.ShapeDtypeStruct((m // tm, hrows, inner), F32)],
        scratch_shapes=[pltpu.VMEM((hrows + tm, inner), F32)],
        compiler_params=_cparams("arbitrary"),
        name="ml_in",
    )(*args)


def _ml_cell_kernel(q_ref, k_ref, v_ref, gate_ref, sg_ref, gsk_ref, gn_ref, c0_ref, n0_ref, m0_ref,
                    y_ref, co_ref, no_ref, mo_ref, c_sc, n_sc, m_sc, *, t_valid):
    h = pl.program_id(1)
    c = pl.program_id(2)
    nc = pl.num_programs(2)
    ln = q_ref.shape[1]

    @pl.when(c == 0)
    def _():
        c_sc[...] = c0_ref[0, 0]
        n_sc[...] = n0_ref[0, 0]
        m_sc[...] = m0_ref[0, 0]

    q = q_ref[0]
    ks = k_ref[0]
    v = v_ref[0]
    gates = gate_ref[0]
    glane = lax.broadcasted_iota(jnp.int32, gates.shape, 1)
    i_col = jnp.sum(jnp.where(glane == h, gates, 0.0), axis=1, keepdims=True)
    f_col = jnp.sum(jnp.where(glane == h + ML_HEADS, gates, 0.0), axis=1, keepdims=True)
    fl_col = _log_sigmoid(f_col)
    if t_valid is not None:
        live = (c * ln + lax.broadcasted_iota(jnp.int32, (ln, 1), 0)) < t_valid
        i_col = jnp.where(live, i_col, NEG)
        fl_col = jnp.where(live, fl_col, 0.0)

    s_i = lax.broadcasted_iota(jnp.int32, (ln, ln), 0)
    r_i = lax.broadcasted_iota(jnp.int32, (ln, ln), 1)
    causal = r_i <= s_i
    eye = r_i == s_i
    to_row = lambda col: jnp.sum(jnp.where(eye, col, 0.0), axis=0, keepdims=True)
    fl_row = to_row(fl_col)
    i_row = to_row(i_col)
    b_col = jnp.sum(jnp.where(causal, fl_row, 0.0), axis=1, keepdims=True)
    b_row = to_row(b_col)
    m_old = m_sc[:, 0:1]

    logw = jnp.where(causal, b_col - b_row + i_row, NEG)
    inter = b_col + m_old
    mt = jnp.maximum(inter, jnp.max(logw, axis=1, keepdims=True))
    scores = lax.dot_general(q, ks, (((1,), (1,)), ((), ())), preferred_element_type=F32)
    a = jnp.exp(logw - mt) * scores
    si = jnp.exp(inter - mt)
    cmat = c_sc[...]
    nvec = n_sc[...]
    num = si * jnp.dot(q, cmat.astype(BF16), preferred_element_type=F32) \
        + jnp.dot(a.astype(BF16), v, preferred_element_type=F32)
    qn = jnp.sum(q.astype(F32) * nvec, axis=1, keepdims=True)
    den = si * qn + jnp.sum(a, axis=1, keepdims=True)
    hc = num / jnp.maximum(jnp.abs(den), jnp.exp(-mt))

    mu = jnp.mean(hc, axis=1, keepdims=True)
    dev = hc - mu
    var = jnp.mean(dev * dev, axis=1, keepdims=True)
    hn = dev * lax.rsqrt(var + LN_EPS) * gn_ref[...]
    y_ref[0] = (hn * sg_ref[0].astype(F32) + gsk_ref[0].astype(F32)).astype(y_ref.dtype)

    b_last = b_col[ln - 1:ln, :]
    wr = b_last - b_col + i_col
    m_new = jnp.maximum(b_last + m_old, jnp.max(wr, axis=0, keepdims=True))
    wk = jnp.exp(wr - m_new) * ks.astype(F32)
    sc = jnp.exp(b_last + m_old - m_new)
    upd = lax.dot_general(wk.astype(BF16), v, (((0,), (0,)), ((), ())), preferred_element_type=F32)
    c_new = sc * cmat + upd
    n_new = sc * nvec + jnp.sum(wk, axis=0, keepdims=True)
    c_sc[...] = c_new
    n_sc[...] = n_new
    m_sc[...] = jnp.broadcast_to(m_new, m_sc.shape)

    @pl.when(c == nc - 1)
    def _():
        co_ref[0, 0] = c_new
        no_ref[0, 0] = n_new
        mo_ref[0, 0] = jnp.broadcast_to(m_new, m_sc.shape)


def _ml_cell(q, k, v, gates, sg, gsk, gnorm, c0, n0, m0, *, chunk, t_valid):
    n, t, inner = q.shape
    hd = inner // ML_HEADS
    nc = t // chunk
    tok = pl.BlockSpec((1, chunk, hd), lambda b, h, c: (b, c, h))
    st = lambda r, w: pl.BlockSpec((1, 1, r, w), lambda b, h, c: (b, h, 0, 0))
    lanes = m0.shape[-1]
    return pl.pallas_call(
        functools.partial(_ml_cell_kernel, t_valid=t_valid),
        grid=(n, ML_HEADS, nc),
        in_specs=[tok, tok, tok, pl.BlockSpec((1, chunk, gates.shape[-1]), lambda b, h, c: (b, c, 0)),
                  tok, tok, pl.BlockSpec((1, hd), lambda b, h, c: (0, h)),
                  st(hd, hd), st(1, hd), st(1, lanes)],
        out_specs=[tok, st(hd, hd), st(1, hd), st(1, lanes)],
        out_shape=[jax.ShapeDtypeStruct((n, t, inner), BF16),
                   jax.ShapeDtypeStruct((n, ML_HEADS, hd, hd), F32),
                   jax.ShapeDtypeStruct((n, ML_HEADS, 1, hd), F32),
                   jax.ShapeDtypeStruct((n, ML_HEADS, 1, lanes), F32)],
        scratch_shapes=[pltpu.VMEM((hd, hd), F32), pltpu.VMEM((1, hd), F32), pltpu.VMEM((1, lanes), F32)],
        compiler_params=_cparams("arbitrary", "arbitrary", "arbitrary"),
        name="ml_cell",
    )(q, k, v, gates, sg, gsk, gnorm, c0, n0, m0)


def _proj_res_kernel(y_ref, w_ref, res_ref, o_ref):
    o_ref[...] = res_ref[...] + jnp.dot(y_ref[...], w_ref[...], preferred_element_type=F32)


def _proj_res(y, w, res, tm):
    m, kdim = y.shape
    d = res.shape[1]
    return pl.pallas_call(
        _proj_res_kernel,
        grid=(m // tm,),
        in_specs=[pl.BlockSpec((tm, kdim), lambda i: (i, 0)), _const_spec(w.shape),
                  pl.BlockSpec((tm, d), lambda i: (i, 0))],
        out_specs=pl.BlockSpec((tm, d), lambda i: (i, 0)),
        out_shape=jax.ShapeDtypeStruct((m, d), F32),
        compiler_params=_cparams("arbitrary"),
        name="ml_out",
    )(y, w, res)


def _headwise_dense(w):
    per = HEADWISE_CHUNK // ML_QKV_BLOCK
    w4 = w.reshape(-1, per, ML_QKV_BLOCK, ML_QKV_BLOCK)
    eye = jnp.eye(per, dtype=w.dtype)
    dense = w4[:, :, :, None, :] * eye[None, :, None, :, None]
    return dense.reshape(-1, HEADWISE_CHUNK, HEADWISE_CHUNK).astype(BF16)


def _time_major(t):
    return jnp.swapaxes(t, 0, 1).reshape(-1, t.shape[-1])


def _seq_major(t, n):
    return jnp.swapaxes(t.reshape(-1, n, t.shape[-1]), 0, 1)


def kernel(x_prompt, x_sample, cache_a_k, cache_a_v, state_pool, state_ml_C, state_ml_n, state_ml_m, state_ml_conv, norm_mix, norm_ffn, norm_final, ab_w_in, ab_w_pool, ab_pool_scale, ab_w_out, ml_w_in, ml_w_conv, ml_b_conv, ml_w_q, ml_w_k, ml_w_v, ml_w_i, ml_b_i, ml_w_f, ml_b_f, ml_norm, ml_skip, ml_w_out, ffn_w1, ffn_w2):
    b, s, d = x_prompt.shape
    ns, ts, _ = x_sample.shape
    aw = A_HEADS * HEAD_DIM
    inner = ml_w_conv.shape[-1]
    hd = inner // ML_HEADS
    tm = 512
    row1 = lambda t: t.reshape(1, -1)

    hp = x_prompt.reshape(b * s, d)
    hs = _time_major(x_sample)
    tms = hs.shape[0]

    w_in = ab_w_in[0].astype(BF16)
    wp = ab_w_pool[0].astype(BF16)
    psc = row1(ab_pool_scale[0])
    wo = ab_w_out[0].astype(BF16)
    wo_a, wo_b = wo[:aw], wo[aw:]
    g0 = row1(norm_mix[0])

    q, kf, vf, kb, vb, u = _ab_in(hp, g0, w_in, tm)
    to_seq = lambda t: t.reshape(b, s, aw)
    a_p = _attn(to_seq(q), to_seq(kb), to_seq(vb)).reshape(b * s, aw)
    tps = s // tm
    halo_spec = pl.BlockSpec((POOL_HALO, u.shape[1]), lambda i: (jnp.maximum(i * (tm // POOL_HALO) - 1, 0), 0))
    hp = _ab_out(a_p, u, u, halo_spec, wp, psc, wo_a, wo_b, hp,
                 tm=tm, g_seq=1, tiles_per_seq=tps, start=0, fresh=True)
    a_rows = min(A_STEPS * max(A_DILATIONS), s)
    heads = lambda t: t.reshape(t.shape[0], -1, A_HEADS, HEAD_DIM)
    p_ak = heads(to_seq(kf)[:, s - a_rows:])[None]
    p_av = heads(to_seq(vf)[:, s - a_rows:])[None]
    pool_buf = state_pool.shape[2]
    p_pool = u.reshape(b, s, -1)[:, s - pool_buf:][None]

    qs, kfs, vfs, _, _, us = _ab_in(hs, g0, w_in, tms)
    s_ak = heads(_seq_major(kfs, ns))
    s_av = heads(_seq_major(vfs, ns))
    head_major = lambda t: jnp.pad(jnp.swapaxes(t, 1, 2), ((0, 0), (0, 0), (0, 8 - ts), (0, 0)))
    a_s = _sattn(head_major(heads(_seq_major(qs.astype(F32), ns))), head_major(s_ak),
                 jnp.transpose(s_av, (0, 2, 3, 1)), jnp.transpose(cache_a_k[0], (0, 2, 3, 1)),
                 jnp.transpose(cache_a_v[0], (0, 2, 3, 1)), ts)
    s_ak, s_av = s_ak[None], s_av[None]
    a_s = jnp.transpose(a_s, (3, 0, 1, 2)).reshape(ts * ns, aw)
    halo_s = jnp.pad(_time_major(state_pool[0]), ((ns * (POOL_HALO - pool_buf), 0), (0, 0)))
    hs = _ab_out(a_s, us, halo_s, _const_spec(halo_s.shape), wp, psc, wo_a, wo_b, hs,
                 tm=tms, g_seq=ns, tiles_per_seq=1, start=PAST_LEN, fresh=False)
    s_pool = jnp.concatenate([state_pool[0], _seq_major(us, ns)], axis=1)[:, ts:][None]

    w1 = ffn_w1[0].astype(BF16)
    w2 = ffn_w2[0].astype(BF16)
    gf0 = row1(norm_ffn[0])
    hp = _ffn(hp, gf0, w1, w2, None, tm)
    hs = _ffn(hs, gf0, w1, w2, None, tms)

    g1 = row1(norm_mix[1])
    wi = ml_w_in[0].astype(BF16)
    wq, wk, wv = (_headwise_dense(t[0]) for t in (ml_w_q, ml_w_k, ml_w_v))
    glanes = 128
    wg = jnp.pad(jnp.concatenate([ml_w_i[0], ml_w_f[0]], axis=1), ((0, 0), (0, glanes - 2 * ML_HEADS))).astype(BF16)
    bg = jnp.pad(jnp.concatenate([ml_b_i[0], ml_b_f[0]]), (0, glanes - 2 * ML_HEADS)).reshape(1, glanes)
    consts = (ml_w_conv[0], row1(ml_b_conv[0]), wq, wk, wv, wg, bg, row1(ml_skip[0]))
    gn = row1(ml_norm[0])
    wout = ml_w_out[0].astype(BF16)
    conv_buf = ML_CONV - 1

    tm1 = 256
    halo_p = 8
    q1, k1, v1, gt, sg, gsk, tail = _ml_in(hp, g1, wi, None, *consts, tm=tm1, g_seq=1,
                                           tiles_per_seq=s // tm1, halo_steps=halo_p)
    seq3 = lambda t: t.reshape(b, s, -1)
    zeros_state = (jnp.zeros((b, ML_HEADS, hd, hd), F32), jnp.zeros((b, ML_HEADS, 1, hd), F32),
                   jnp.zeros((b, ML_HEADS, 1, 128), F32))
    y, p_c, p_n, p_m = _ml_cell(seq3(q1), seq3(k1), seq3(v1), seq3(gt), seq3(sg), seq3(gsk), gn, *zeros_state,
                                chunk=ML_PROMPT_CHUNK, t_valid=None)
    hp = _proj_res(y.reshape(b * s, inner), wout, hp, tm)
    p_conv = tail.reshape(b, s // tm1, halo_p, inner)[:, -1, halo_p - conv_buf:][None]

    halo_c = jnp.pad(_time_major(state_ml_conv[0]), ((ns, 0), (0, 0)))
    q1, k1, v1, gt, sg, gsk, tail = _ml_in(hs, g1, wi, halo_c, *consts, tm=tms, g_seq=ns,
                                           tiles_per_seq=1, halo_steps=conv_buf + 1)
    pad16 = lambda t: jnp.pad(_seq_major(t, ns), ((0, 0), (0, ML_SAMPLE_PAD - ts), (0, 0)))
    m0 = jnp.broadcast_to(state_ml_m[0][:, :, None, None], (ns, ML_HEADS, 1, 128))
    ys, s_c, s_n, s_m = _ml_cell(pad16(q1), pad16(k1), pad16(v1), pad16(gt), pad16(sg), pad16(gsk), gn,
                                 state_ml_C[0], state_ml_n[0][:, :, None, :], m0,
                                 chunk=ML_SAMPLE_PAD, t_valid=ts)
    hs = _proj_res(_time_major(ys[:, :ts]), wout, hs, tms)
    s_conv = _seq_major(tail[0], ns)[:, -conv_buf:][None]

    w1 = ffn_w1[1].astype(BF16)
    w2 = ffn_w2[1].astype(BF16)
    gf1 = row1(norm_ffn[1])
    gfin = row1(norm_final)
    y_prompt = _ffn(hp, gf1, w1, w2, gfin, tm).reshape(b, s, d)
    y_sample = _seq_major(_ffn(hs, gf1, w1, w2, gfin, tms), ns)

    return (y_prompt, y_sample, p_ak, p_av, p_pool,
            p_c[None], p_n[:, :, 0][None], p_m[:, :, 0, 0][None], p_conv,
            s_ak, s_av, s_pool,
            s_c[None], s_n[:, :, 0][None], s_m[:, :, 0, 0][None], s_conv)
```

```python
import functools

import jax
import jax.numpy as jnp
from jax import lax
from jax.experimental import pallas as pl
from jax.experimental.pallas import tpu as pltpu

F32 = jnp.float32
BF16 = jnp.bfloat16

PAST_LEN = 16384
A_HEADS = 8
HEAD_DIM = 64
A_DILATIONS = (1, 4, 16)
A_STEPS = 128
A_BLK = 128
A_SUPER = A_BLK * max(A_DILATIONS)
ATTN_SCALE = HEAD_DIM ** -0.5
POOL_SIZES = (2, 4, 8, 16)
POOL_HALO = 16
ML_HEADS = 4
ML_CONV = 4
ML_QKV_BLOCK = 4
ML_PROMPT_CHUNK = 256
ML_SAMPLE_PAD = 16
ML_HEADS_PER_STEP = 2
ML_NCOLS = 128
ML_SLAB = 128
FFN_CHUNK = 1024
HEADWISE_CHUNK = 256
RMS_EPS = 1e-6
LN_EPS = 1e-5
NEG = -1e30
VMEM_LIMIT = 56 * 1024 * 1024


def _cparams(*sem):
    return pltpu.CompilerParams(dimension_semantics=sem, vmem_limit_bytes=VMEM_LIMIT)


def _const_spec(shape):
    nd = len(shape)
    return pl.BlockSpec(shape, lambda *_: (0,) * nd, pipeline_mode=pl.Buffered(1))


def _rms(x, g):
    return x * lax.rsqrt(jnp.mean(x * x, axis=-1, keepdims=True) + RMS_EPS) * g


def _log_sigmoid(x):
    return jnp.minimum(x, 0.0) - jnp.log(1.0 + jnp.exp(-jnp.abs(x)))


def _ab_in_kernel(x_ref, g_ref, w_ref, q_ref, kf_ref, vf_ref, kb_ref, vb_ref, u_ref):
    aw = q_ref.shape[-1]
    xn = _rms(x_ref[...], g_ref[...]).astype(BF16)
    p = jnp.dot(xn, w_ref[...], preferred_element_type=F32)
    q_ref[...] = (p[:, :aw] * ATTN_SCALE).astype(BF16)
    k = p[:, aw:2 * aw]
    v = p[:, 2 * aw:3 * aw]
    kf_ref[...] = k
    vf_ref[...] = v
    kb_ref[...] = k.astype(BF16)
    vb_ref[...] = v.astype(BF16)
    u_ref[...] = p[:, 3 * aw:]


def _ab_in(x, g, w, tm):
    m, d = x.shape
    aw = A_HEADS * HEAD_DIM
    bw = w.shape[1] - 3 * aw
    row = lambda n: pl.BlockSpec((tm, n), lambda i: (i, 0))
    return pl.pallas_call(
        _ab_in_kernel,
        grid=(m // tm,),
        in_specs=[row(d), _const_spec((1, d)), _const_spec(w.shape)],
        out_specs=[row(aw), row(aw), row(aw), row(aw), row(aw), row(bw)],
        out_shape=[jax.ShapeDtypeStruct((m, aw), BF16), jax.ShapeDtypeStruct((m, aw), F32),
                   jax.ShapeDtypeStruct((m, aw), F32), jax.ShapeDtypeStruct((m, aw), BF16),
                   jax.ShapeDtypeStruct((m, aw), BF16), jax.ShapeDtypeStruct((m, bw), F32)],
        compiler_params=_cparams("arbitrary"),
        name="ab_in",
    )(x, g, w)


def _attn_kernel(q_ref, kc_ref, kp_ref, vc_ref, vp_ref, a_ref, qs, ks, vs, os_, ms_, ds_, bias):
    sb = pl.program_id(1)
    qs[...] = q_ref[0].astype(F32)
    ks[0:A_SUPER, :] = kp_ref[0].astype(F32)
    ks[A_SUPER:, :] = kc_ref[0].astype(F32)
    vs[0:A_SUPER, :] = vp_ref[0].astype(F32)
    vs[A_SUPER:, :] = vc_ref[0].astype(F32)
    qi = lax.broadcasted_iota(jnp.int32, (A_BLK, 2 * A_BLK), 0)
    ki = lax.broadcasted_iota(jnp.int32, (A_BLK, 2 * A_BLK), 1)
    dist = qi - ki + A_BLK
    band = (dist >= 0) & (dist <= A_STEPS)
    bias[0] = jnp.where(band, 0.0, NEG)
    bias[1] = jnp.where(band & (ki >= A_BLK), 0.0, NEG)
    lane = lax.broadcasted_iota(jnp.int32, (A_BLK, 2 * HEAD_DIM), 1)
    low = lane < HEAD_DIM
    nt = (((1,), (1,)), ((), ()))

    for g, dil in enumerate(A_DILATIONS):
        blocks = A_SUPER // (A_BLK * dil)

        def body(idx, carry, g=g, dil=dil, blocks=blocks):
            r = idx // blocks
            n = idx % blocks
            q0 = n * (A_BLK * dil) + r
            k0 = A_SUPER + (n - 1) * (A_BLK * dil) + r
            if dil == 1:
                q0 = pl.multiple_of(q0, A_BLK)
                k0 = pl.multiple_of(k0, A_BLK)
                rows_q = pl.ds(q0, A_BLK)
                rows_k = pl.ds(k0, 2 * A_BLK)
            else:
                rows_q = pl.ds(q0, A_BLK, stride=dil)
                rows_k = pl.ds(k0, 2 * A_BLK, stride=dil)
            qp = qs[rows_q, :]
            kp = ks[rows_k, :].astype(BF16)
            vp = vs[rows_k, :].astype(BF16)
            mask_bias = bias[((n == 0) & (sb == 0)).astype(jnp.int32)]
            accs, maxs, dens = [], [], []
            for j in range(2):
                qm = jnp.where(low if j == 0 else ~low, qp, 0.0).astype(BF16)
                s = lax.dot_general(qm, kp, nt, preferred_element_type=F32) + mask_bias
                m = jnp.max(s, axis=-1, keepdims=True)
                e = jnp.exp(s - m)
                maxs.append(m)
                dens.append(jnp.sum(e, axis=-1, keepdims=True))
                accs.append(jnp.dot(e.astype(BF16), vp, preferred_element_type=F32))
            os_[g, rows_q, :] = jnp.where(low, accs[0], accs[1])
            ms_[g, rows_q, :] = jnp.where(low, maxs[0], maxs[1])
            ds_[g, rows_q, :] = jnp.where(low, dens[0], dens[1])
            return carry

        lax.fori_loop(0, A_SUPER // A_BLK, body, 0, unroll=4)

    ms = [ms_[g] for g in range(len(A_DILATIONS))]
    mm = functools.reduce(jnp.maximum, ms)
    es = [jnp.exp(m - mm) for m in ms]
    num = sum(e * os_[g] for g, e in enumerate(es))
    den = sum(e * ds_[g] for g, e in enumerate(es))
    a_ref[0] = (num / den).astype(a_ref.dtype)


def _attn(q, k, v):
    b, s, aw = q.shape
    pw = 2 * HEAD_DIM
    assert s % A_SUPER == 0
    cur = pl.BlockSpec((1, A_SUPER, pw), lambda bi, sb, p: (bi, sb, p))
    prev = pl.BlockSpec((1, A_SUPER, pw), lambda bi, sb, p: (bi, jnp.maximum(sb - 1, 0), p))
    nd = len(A_DILATIONS)
    return pl.pallas_call(
        _attn_kernel,
        grid=(b, s // A_SUPER, aw // pw),
        in_specs=[cur, cur, prev, cur, prev],
        out_specs=cur,
        out_shape=jax.ShapeDtypeStruct((b, s, aw), BF16),
        scratch_shapes=[pltpu.VMEM((A_SUPER, pw), F32), pltpu.VMEM((2 * A_SUPER, pw), F32),
                        pltpu.VMEM((2 * A_SUPER, pw), F32), pltpu.VMEM((nd, A_SUPER, pw), F32),
                        pltpu.VMEM((nd, A_SUPER, pw), F32), pltpu.VMEM((nd, A_SUPER, pw), F32),
                        pltpu.VMEM((2, A_BLK, 2 * A_BLK), F32)],
        compiler_params=_cparams("arbitrary", "arbitrary", "arbitrary"),
        name="attn",
    )(q, k, k, v, v)


def _sattn_kernel(q_ref, kn_ref, vnt_ref, kt_ref, vt_ref, a_ref, *, t_len):
    nh, tp_rows, _ = q_ref.shape[1:]
    buf = kt_ref.shape[-1]
    rnd = lambda x: x.astype(BF16).astype(F32)
    trow = lax.broadcasted_iota(jnp.int32, (tp_rows, 1), 0)
    biases = []
    for dil in A_DILATIONS:
        span = A_STEPS * dil
        delta = (span + lax.broadcasted_iota(jnp.int32, (tp_rows, span), 0)
                 - lax.broadcasted_iota(jnp.int32, (tp_rows, span), 1))
        biases.append(jnp.where((delta % dil == 0) & (delta <= span), 0.0, NEG))
    for h in range(nh):
        qh = q_ref[0, h]
        knh = rnd(kn_ref[0, h])
        vnt = rnd(vnt_ref[0, h])
        vt = rnd(vt_ref[0, h])
        s_all = jnp.dot(qh.astype(BF16), kt_ref[0, h].astype(BF16), preferred_element_type=F32)
        s_new = [jnp.sum(qh * knh[tp:tp + 1, :], axis=-1, keepdims=True) for tp in range(t_len)]
        outs, lses = [], []
        for dil, bias in zip(A_DILATIONS, biases):
            lo = buf - A_STEPS * dil
            s = s_all[:, lo:] + bias
            s_n = [jnp.where((trow >= tp) & ((trow - tp) % dil == 0), x, NEG) for tp, x in enumerate(s_new)]
            m = functools.reduce(jnp.maximum, s_n, jnp.max(s, axis=-1, keepdims=True))
            p = jnp.exp(s - m)
            p_n = [jnp.exp(x - m) for x in s_n]
            den = jnp.sum(p, axis=-1, keepdims=True) + sum(p_n)
            pr = rnd(p)
            cols = []
            for t in range(t_len):
                acc = jnp.sum(pr[t:t + 1, :] * vt[:, lo:], axis=-1, keepdims=True)
                for tp in range(t + 1):
                    if (t - tp) % dil == 0:
                        acc = acc + rnd(p_n[tp][t:t + 1, :]) * vnt[:, tp:tp + 1]
                cols.append(acc / den[t:t + 1, :])
            outs.append(cols)
            lses.append(m + jnp.log(den))
        mm = functools.reduce(jnp.maximum, lses)
        es = [jnp.exp(l - mm) for l in lses]
        tot = sum(es)
        a_ref[0, h] = jnp.concatenate(
            [sum(e[t:t + 1, :] * o[t] for e, o in zip(es, outs)) / tot[t:t + 1, :] for t in range(t_len)], axis=1)


def _sattn(q, kn, vnt, kt, vt, t_len):
    n, nh, _, hd = q.shape
    buf = kt.shape[-1]
    assert buf >= A_STEPS * max(A_DILATIONS)
    spec = lambda t: pl.BlockSpec((1,) + t.shape[1:], lambda i: (i, 0, 0, 0))
    return pl.pallas_call(
        functools.partial(_sattn_kernel, t_len=t_len),
        grid=(n,),
        in_specs=[spec(t) for t in (q, kn, vnt, kt, vt)],
        out_specs=spec(vnt),
        out_shape=jax.ShapeDtypeStruct(vnt.shape, F32),
        compiler_params=_cparams("arbitrary"),
        name="sattn",
    )(q, kn, vnt, kt, vt)


def _ab_out_kernel(a_ref, u_ref, halo_ref, wp_ref, sc_ref, woa_ref, wob_ref, res_ref, g_ref, w1_ref, w2_ref,
                   h_ref, ext_ref, *, g_seq, tiles_per_seq, start, fresh):
    tm, bw = u_ref.shape
    hrows = POOL_HALO * g_seq
    i = pl.program_id(0)
    a = a_ref[...]
    u = u_ref[...]
    halo = halo_ref[...]
    if fresh:
        halo = jnp.where(i % tiles_per_seq == 0, 0.0, halo)
    ext_ref[0:hrows, :] = halo
    ext_ref[hrows:hrows + tm, :] = u
    t_idx = lax.broadcasted_iota(jnp.int32, (tm, 1), 0) // g_seq
    pos = start + (i % tiles_per_seq) * (tm // g_seq) + t_idx
    gw = bw // len(POOL_SIZES)
    y = jnp.zeros((tm, woa_ref.shape[1]), F32)
    y += jnp.dot(a.astype(BF16), woa_ref[...], preferred_element_type=F32)
    for g, w in enumerate(POOL_SIZES):
        cols = slice(g * gw, (g + 1) * gw)
        acc = u[:, cols]
        for j in range(1, w):
            acc = acc + ext_ref[hrows - j * g_seq:hrows - j * g_seq + tm, cols]
        cnt = jnp.minimum(pos + 1, w).astype(F32)
        pooled = acc / cnt - u[:, cols]
        yg = jnp.dot(pooled.astype(BF16), wp_ref[g], preferred_element_type=F32) * sc_ref[:, cols]
        y += jnp.dot(yg.astype(BF16), wob_ref[cols, :], preferred_element_type=F32)
    h_ref[...] = _ffn_tail(res_ref[...] + y, g_ref, w1_ref, w2_ref, None)


def _ab_out(a, u, halo, halo_spec, wp, scale, wo_a, wo_b, res, g, w1, w2, *, tm, g_seq, tiles_per_seq, start,
            fresh):
    m, bw = u.shape
    d = res.shape[1]
    row = lambda n: pl.BlockSpec((tm, n), lambda i: (i, 0))
    kern = functools.partial(_ab_out_kernel, g_seq=g_seq, tiles_per_seq=tiles_per_seq, start=start, fresh=fresh)
    return pl.pallas_call(
        kern,
        grid=(m // tm,),
        in_specs=[row(a.shape[1]), row(bw), halo_spec, _const_spec(wp.shape), _const_spec(scale.shape),
                  _const_spec(wo_a.shape), _const_spec(wo_b.shape), row(d), _const_spec(g.shape),
                  _const_spec(w1.shape), _const_spec(w2.shape)],
        out_specs=row(d),
        out_shape=jax.ShapeDtypeStruct((m, d), F32),
        scratch_shapes=[pltpu.VMEM((POOL_HALO * g_seq + tm, bw), F32)],
        compiler_params=_cparams("arbitrary"),
        name="ab_out",
    )(a, u, halo, wp, scale, wo_a, wo_b, res, g, w1, w2)


def _ffn_tail(x, g_ref, w1_ref, w2_ref, gf_ref):
    xn = _rms(x, g_ref[...]).astype(BF16)
    acc = jnp.zeros(x.shape, F32)
    for c in range(w1_ref.shape[1] // FFN_CHUNK):
        cols = slice(c * FFN_CHUNK, (c + 1) * FFN_CHUNK)
        h = jnp.maximum(jnp.dot(xn, w1_ref[:, cols], preferred_element_type=F32), 0.0)
        acc += jnp.dot((h * h).astype(BF16), w2_ref[cols, :], preferred_element_type=F32)
    out = x + acc
    if gf_ref is not None:
        out = _rms(out, gf_ref[...])
    return out


def _ml_in_kernel(*refs, g_seq, tiles_per_seq, halo_steps, carry):
    if carry:
        (x_ref, g_ref, win_ref, wc_ref, bc_ref, wq_ref, wk_ref, wv_ref, wg_ref, bg_ref, skip_ref,
         q_ref, k_ref, v_ref, gate_ref, grow_ref, sg_ref, gsk_ref, tail_ref, ext_ref) = refs
    else:
        (x_ref, g_ref, win_ref, halo_ref, wc_ref, bc_ref, wq_ref, wk_ref, wv_ref, wg_ref, bg_ref, skip_ref,
         q_ref, k_ref, v_ref, gate_ref, grow_ref, sg_ref, gsk_ref, tail_ref, ext_ref) = refs
    tm = x_ref.shape[0]
    inner = q_ref.shape[1]
    hd = inner // ML_HEADS
    hrows = halo_steps * g_seq
    i = pl.program_id(0)

    xn = _rms(x_ref[...], g_ref[...]).astype(BF16)
    xm = jnp.dot(xn, win_ref[:, :inner], preferred_element_type=F32)
    og = jnp.dot(xn, win_ref[:, inner:], preferred_element_type=F32)

    if carry:
        @pl.when(i % tiles_per_seq == 0)
        def _():
            ext_ref[0:hrows, :] = jnp.zeros((hrows, inner), F32)
    else:
        ext_ref[0:hrows, :] = halo_ref[...]
    ext_ref[hrows:hrows + tm, :] = xm
    conv = xm * wc_ref[ML_CONV - 1:ML_CONV, :] + bc_ref[...]
    for j in range(ML_CONV - 1):
        off = hrows - (ML_CONV - 1 - j) * g_seq
        conv = conv + ext_ref[off:off + tm, :] * wc_ref[j:j + 1, :]
    tail = ext_ref[tm:tm + hrows, :]
    tail_ref[0] = tail
    if carry:
        ext_ref[0:hrows, :] = tail
    ca = conv * jax.nn.sigmoid(conv)
    sig = jax.nn.sigmoid(og)
    sg_ref[...] = sig.astype(BF16)
    gsk_ref[...] = (skip_ref[...] * ca * sig).astype(BF16)

    ca_b = ca.astype(BF16)
    xm_b = xm.astype(BF16)
    gates = jnp.zeros((tm, wg_ref.shape[1]), F32) + bg_ref[...]
    cw = HEADWISE_CHUNK
    for c in range(inner // cw):
        cols = slice(c * cw, (c + 1) * cw)
        qc = jnp.dot(ca_b[:, cols], wq_ref[c], preferred_element_type=F32)
        kc = jnp.dot(ca_b[:, cols], wk_ref[c], preferred_element_type=F32)
        vc = jnp.dot(xm_b[:, cols], wv_ref[c], preferred_element_type=F32)
        qb, kb, vb = qc.astype(BF16), kc.astype(BF16), vc.astype(BF16)
        q_ref[:, cols] = qb
        k_ref[:, cols] = (kc * hd ** -0.5).astype(BF16)
        v_ref[:, cols] = vb
        gates += jnp.dot(qb, wg_ref[c * cw:(c + 1) * cw, :], preferred_element_type=F32)
        gates += jnp.dot(kb, wg_ref[inner + c * cw:inner + (c + 1) * cw, :], preferred_element_type=F32)
        gates += jnp.dot(vb, wg_ref[2 * inner + c * cw:2 * inner + (c + 1) * cw, :], preferred_element_type=F32)
    lane = lax.broadcasted_iota(jnp.int32, gates.shape, 1)
    row = lax.broadcasted_iota(jnp.int32, gates.shape, 0)
    is_f = (lane >= ML_HEADS) & (lane < 2 * ML_HEADS)
    b = jnp.where(is_f, _log_sigmoid(gates), 0.0)
    shift = g_seq
    while shift < tm:
        b = b + jnp.where(row >= shift, pltpu.roll(b, shift, axis=0), 0.0)
        shift *= 2
    packed = jnp.where(is_f, b, gates)
    gate_ref[...] = packed
    grow_ref[0] = jnp.transpose(packed)[0:2 * ML_HEADS, :]


def _ml_in(x, g, w_in, halo, wc, bc, wq, wk, wv, wg, bg, skip, *, tm, g_seq, tiles_per_seq, halo_steps):
    m, d = x.shape
    inner = wc.shape[1]
    carry = halo is None
    hrows = halo_steps * g_seq
    row = lambda n: pl.BlockSpec((tm, n), lambda i: (i, 0))
    consts = [wc, bc, wq, wk, wv, wg, bg, skip]
    args = [x, g, w_in] + ([] if carry else [halo]) + consts
    specs = ([row(d), _const_spec(g.shape), _const_spec(w_in.shape)]
             + ([] if carry else [_const_spec(halo.shape)]) + [_const_spec(t.shape) for t in consts])
    big = lambda dt: jax.ShapeDtypeStruct((m, inner), dt)
    return pl.pallas_call(
        functools.partial(_ml_in_kernel, g_seq=g_seq, tiles_per_seq=tiles_per_seq, halo_steps=halo_steps,
                          carry=carry),
        grid=(m // tm,),
        in_specs=specs,
        out_specs=[row(inner), row(inner), row(inner), row(wg.shape[1]),
                   pl.BlockSpec((1, 2 * ML_HEADS, tm), lambda i: (i, 0, 0)), row(inner), row(inner),
                   pl.BlockSpec((1, hrows, inner), lambda i: (i, 0, 0))],
        out_shape=[big(BF16), big(BF16), big(BF16), jax.ShapeDtypeStruct((m, wg.shape[1]), F32),
                   jax.ShapeDtypeStruct((m // tm, 2 * ML_HEADS, tm), F32),
                   big(BF16), big(BF16), jax.ShapeDtypeStruct((m // tm, hrows, inner), F32)],
        scratch_shapes=[pltpu.VMEM((hrows + tm, inner), F32)],
        compiler_params=_cparams("arbitrary"),
        name="ml_in",
    )(*args)


def _ml_cell_kernel(q_ref, k_ref, v_ref, gcol_ref, grow_ref, c0_ref, n0_ref, m0_ref,
                    h_ref, co_ref, no_ref, mo_ref, cn_sc, m_sc):
    c = pl.program_id(2)
    nc = pl.num_programs(2)
    ln = q_ref.shape[1]
    hd = q_ref.shape[2] // ML_HEADS_PER_STEP
    s_i = lax.broadcasted_iota(jnp.int32, (ln, ln), 0)
    r_i = lax.broadcasted_iota(jnp.int32, (ln, ln), 1)
    causal = r_i <= s_i
    gcol = gcol_ref[0]
    glane = lax.broadcasted_iota(jnp.int32, gcol.shape, 1)

    @pl.when(c == 0)
    def _():
        for j in range(ML_HEADS_PER_STEP):
            cn_sc[j, :, 0:hd] = c0_ref[0, j]
            cn_sc[j, :, hd:] = jnp.transpose(jnp.broadcast_to(n0_ref[0, j], (ML_NCOLS, hd)))
            m_sc[j] = m0_ref[0, j]

    for j in range(ML_HEADS_PER_STEP):
        h = pl.program_id(1) * ML_HEADS_PER_STEP + j
        cols = slice(j * hd, (j + 1) * hd)
        q = q_ref[0, :, cols]
        ks = k_ref[0, :, cols]
        v = v_ref[0, :, cols]
        i_col = jnp.sum(jnp.where(glane == h, gcol, 0.0), axis=1, keepdims=True)
        b_col = jnp.sum(jnp.where(glane == h + ML_HEADS, gcol, 0.0), axis=1, keepdims=True)
        i_row = grow_ref[0, pl.ds(h, 1), :]
        b_row = grow_ref[0, pl.ds(h + ML_HEADS, 1), :]
        m_old = m_sc[j, :, 0:1]

        logw = jnp.where(causal, b_col - b_row + i_row, NEG)
        inter = b_col + m_old
        mt = jnp.maximum(inter, jnp.max(logw, axis=1, keepdims=True))
        scores = lax.dot_general(q, ks, (((1,), (1,)), ((), ())), preferred_element_type=F32)
        a = jnp.exp(logw - mt) * scores
        si = jnp.exp(inter - mt)
        qcn = jnp.dot(q, cn_sc[j].astype(BF16), preferred_element_type=F32)
        num = si * qcn[:, 0:hd] + jnp.dot(a.astype(BF16), v, preferred_element_type=F32)
        den = si * qcn[:, hd:hd + 1] + jnp.sum(a, axis=1, keepdims=True)
        h_ref[0, :, cols] = (num / jnp.maximum(jnp.abs(den), jnp.exp(-mt))).astype(h_ref.dtype)

        b_last = b_col[ln - 1:ln, :]
        wr = b_last - b_col + i_col
        m_new = jnp.maximum(b_last + m_old, jnp.max(wr, axis=0, keepdims=True))
        wk = (jnp.exp(wr - m_new) * ks.astype(F32)).astype(BF16)
        sc = jnp.exp(b_last + m_old - m_new)
        vaug = jnp.concatenate([v, jnp.ones((ln, ML_NCOLS), v.dtype)], axis=1)
        for r in range(hd // ML_SLAB):
            rows = slice(r * ML_SLAB, (r + 1) * ML_SLAB)
            upd = lax.dot_general(wk[:, rows], vaug, (((0,), (0,)), ((), ())), preferred_element_type=F32)
            cn_sc[j, rows, :] = sc * cn_sc[j, rows, :] + upd
        m_sc[j] = jnp.broadcast_to(m_new, m_sc.shape[1:])

    @pl.when(c == nc - 1)
    def _():
        for j in range(ML_HEADS_PER_STEP):
            co_ref[0, j] = cn_sc[j, :, 0:hd]
            no_ref[0, j] = jnp.transpose(cn_sc[j, :, hd:])[0:1, :]
            mo_ref[0, j] = m_sc[j]


def _ml_cell(q, k, v, gcol, grow, c0, n0, m0, *, chunk):
    n, t, inner = q.shape
    hd = inner // ML_HEADS
    hps = ML_HEADS_PER_STEP
    nc = t // chunk
    tok = pl.BlockSpec((1, chunk, hps * hd), lambda b, h, c: (b, c, h))
    st = lambda r, w: pl.BlockSpec((1, hps, r, w), lambda b, h, c: (b, h, 0, 0))
    lanes = m0.shape[-1]
    return pl.pallas_call(
        _ml_cell_kernel,
        grid=(n, ML_HEADS // hps, nc),
        in_specs=[tok, tok, tok, pl.BlockSpec((1, chunk, gcol.shape[-1]), lambda b, h, c: (b, c, 0)),
                  pl.BlockSpec((1, 2 * ML_HEADS, chunk), lambda b, h, c: (b * nc + c, 0, 0)),
                  st(hd, hd), st(1, hd), st(1, lanes)],
        out_specs=[tok, st(hd, hd), st(1, hd), st(1, lanes)],
        out_shape=[jax.ShapeDtypeStruct((n, t, inner), BF16),
                   jax.ShapeDtypeStruct((n, ML_HEADS, hd, hd), F32),
                   jax.ShapeDtypeStruct((n, ML_HEADS, 1, hd), F32),
                   jax.ShapeDtypeStruct((n, ML_HEADS, 1, lanes), F32)],
        scratch_shapes=[pltpu.VMEM((hps, hd, hd + ML_NCOLS), F32), pltpu.VMEM((hps, 1, lanes), F32)],
        compiler_params=_cparams("arbitrary", "arbitrary", "arbitrary"),
        name="ml_cell",
    )(q, k, v, gcol, grow, c0, n0, m0)


def _ml_out_kernel(hc_ref, sg_ref, gsk_ref, gn_ref, w_ref, res_ref, g_ref, w1_ref, w2_ref, gf_ref, o_ref):
    hd = hc_ref.shape[1] // ML_HEADS
    h = res_ref[...]
    for j in range(ML_HEADS):
        cols = slice(j * hd, (j + 1) * hd)
        hc = hc_ref[:, cols].astype(F32)
        dev = hc - jnp.mean(hc, axis=1, keepdims=True)
        var = jnp.mean(dev * dev, axis=1, keepdims=True)
        hn = dev * lax.rsqrt(var + LN_EPS) * gn_ref[:, cols]
        y = hn * sg_ref[:, cols].astype(F32) + gsk_ref[:, cols].astype(F32)
        h = h + jnp.dot(y.astype(BF16), w_ref[cols, :], preferred_element_type=F32)
    o_ref[...] = _ffn_tail(h, g_ref, w1_ref, w2_ref, gf_ref)


def _ml_out(hc, sg, gsk, gn, w, res, g, w1, w2, gf, tm):
    m, kdim = hc.shape
    d = res.shape[1]
    row = lambda n: pl.BlockSpec((tm, n), lambda i: (i, 0))
    return pl.pallas_call(
        _ml_out_kernel,
        grid=(m // tm,),
        in_specs=[row(kdim), row(kdim), row(kdim), _const_spec(gn.shape), _const_spec(w.shape), row(d),
                  _const_spec(g.shape), _const_spec(w1.shape), _const_spec(w2.shape), _const_spec(gf.shape)],
        out_specs=row(d),
        out_shape=jax.ShapeDtypeStruct((m, d), F32),
        compiler_params=_cparams("arbitrary"),
        name="ml_out",
    )(hc, sg, gsk, gn, w, res, g, w1, w2, gf)


def _headwise_dense(w):
    per = HEADWISE_CHUNK // ML_QKV_BLOCK
    w4 = w.reshape(-1, per, ML_QKV_BLOCK, ML_QKV_BLOCK)
    eye = jnp.eye(per, dtype=w.dtype)
    dense = w4[:, :, :, None, :] * eye[None, :, None, :, None]
    return dense.reshape(-1, HEADWISE_CHUNK, HEADWISE_CHUNK).astype(BF16)


def _time_major(t):
    return jnp.swapaxes(t, 0, 1).reshape(-1, t.shape[-1])


def _seq_major(t, n):
    return jnp.swapaxes(t.reshape(-1, n, t.shape[-1]), 0, 1)


def kernel(x_prompt, x_sample, cache_a_k, cache_a_v, state_pool, state_ml_C, state_ml_n, state_ml_m, state_ml_conv, norm_mix, norm_ffn, norm_final, ab_w_in, ab_w_pool, ab_pool_scale, ab_w_out, ml_w_in, ml_w_conv, ml_b_conv, ml_w_q, ml_w_k, ml_w_v, ml_w_i, ml_b_i, ml_w_f, ml_b_f, ml_norm, ml_skip, ml_w_out, ffn_w1, ffn_w2):
    b, s, d = x_prompt.shape
    ns, ts, _ = x_sample.shape
    aw = A_HEADS * HEAD_DIM
    inner = ml_w_conv.shape[-1]
    hd = inner // ML_HEADS
    tm = 512
    row1 = lambda t: t.reshape(1, -1)

    hp = x_prompt.reshape(b * s, d)
    hs = _time_major(x_sample)
    tms = hs.shape[0]

    w_in = ab_w_in[0].astype(BF16)
    wp = ab_w_pool[0].astype(BF16)
    psc = row1(ab_pool_scale[0])
    wo = ab_w_out[0].astype(BF16)
    wo_a, wo_b = wo[:aw], wo[aw:]
    g0 = row1(norm_mix[0])
    ffn0 = (row1(norm_ffn[0]), ffn_w1[0].astype(BF16), ffn_w2[0].astype(BF16))

    q, kf, vf, kb, vb, u = _ab_in(hp, g0, w_in, tm)
    to_seq = lambda t: t.reshape(b, s, aw)
    a_p = _attn(to_seq(q), to_seq(kb), to_seq(vb)).reshape(b * s, aw)
    tps = s // tm
    halo_spec = pl.BlockSpec((POOL_HALO, u.shape[1]), lambda i: (jnp.maximum(i * (tm // POOL_HALO) - 1, 0), 0))
    hp = _ab_out(a_p, u, u, halo_spec, wp, psc, wo_a, wo_b, hp, *ffn0,
                 tm=tm, g_seq=1, tiles_per_seq=tps, start=0, fresh=True)
    a_rows = min(A_STEPS * max(A_DILATIONS), s)
    heads = lambda t: t.reshape(t.shape[0], -1, A_HEADS, HEAD_DIM)
    p_ak = heads(to_seq(kf)[:, s - a_rows:])[None]
    p_av = heads(to_seq(vf)[:, s - a_rows:])[None]
    pool_buf = state_pool.shape[2]
    p_pool = u.reshape(b, s, -1)[:, s - pool_buf:][None]

    qs, kfs, vfs, _, _, us = _ab_in(hs, g0, w_in, tms)
    s_ak = heads(_seq_major(kfs, ns))
    s_av = heads(_seq_major(vfs, ns))
    head_major = lambda t: jnp.pad(jnp.swapaxes(t, 1, 2), ((0, 0), (0, 0), (0, 8 - ts), (0, 0)))
    a_s = _sattn(head_major(heads(_seq_major(qs.astype(F32), ns))), head_major(s_ak),
                 jnp.transpose(s_av, (0, 2, 3, 1)), jnp.transpose(cache_a_k[0], (0, 2, 3, 1)),
                 jnp.transpose(cache_a_v[0], (0, 2, 3, 1)), ts)
    s_ak, s_av = s_ak[None], s_av[None]
    a_s = jnp.transpose(a_s, (3, 0, 1, 2)).reshape(ts * ns, aw)
    halo_s = jnp.pad(_time_major(state_pool[0]), ((ns * (POOL_HALO - pool_buf), 0), (0, 0)))
    hs = _ab_out(a_s, us, halo_s, _const_spec(halo_s.shape), wp, psc, wo_a, wo_b, hs, *ffn0,
                 tm=tms, g_seq=ns, tiles_per_seq=1, start=PAST_LEN, fresh=False)
    s_pool = jnp.concatenate([state_pool[0], _seq_major(us, ns)], axis=1)[:, ts:][None]

    g1 = row1(norm_mix[1])
    wi = ml_w_in[0].astype(BF16)
    wq, wk, wv = (_headwise_dense(t[0]) for t in (ml_w_q, ml_w_k, ml_w_v))
    glanes = 128
    wg = jnp.pad(jnp.concatenate([ml_w_i[0], ml_w_f[0]], axis=1), ((0, 0), (0, glanes - 2 * ML_HEADS))).astype(BF16)
    bg = jnp.pad(jnp.concatenate([ml_b_i[0], ml_b_f[0]]), (0, glanes - 2 * ML_HEADS)).reshape(1, glanes)
    consts = (ml_w_conv[0], row1(ml_b_conv[0]), wq, wk, wv, wg, bg, row1(ml_skip[0]))
    gn = row1(ml_norm[0])
    wout = ml_w_out[0].astype(BF16)
    conv_buf = ML_CONV - 1

    w1 = ffn_w1[1].astype(BF16)
    w2 = ffn_w2[1].astype(BF16)
    gf1 = row1(norm_ffn[1])
    gfin = row1(norm_final)

    tm1 = ML_PROMPT_CHUNK
    halo_p = 8
    q1, k1, v1, gc, gr, sg, gsk, tail = _ml_in(hp, g1, wi, None, *consts, tm=tm1, g_seq=1,
                                               tiles_per_seq=s // tm1, halo_steps=halo_p)
    seq3 = lambda t: t.reshape(b, s, -1)
    zeros_state = (jnp.zeros((b, ML_HEADS, hd, hd), F32), jnp.zeros((b, ML_HEADS, 1, hd), F32),
                   jnp.zeros((b, ML_HEADS, 1, 128), F32))
    hc, p_c, p_n, p_m = _ml_cell(seq3(q1), seq3(k1), seq3(v1), seq3(gc), gr, *zeros_state, chunk=ML_PROMPT_CHUNK)
    y_prompt = _ml_out(hc.reshape(b * s, inner), sg, gsk, gn, wout, hp, gf1, w1, w2, gfin, tm).reshape(b, s, d)
    p_conv = tail.reshape(b, s // tm1, halo_p, inner)[:, -1, halo_p - conv_buf:][None]

    halo_c = jnp.pad(_time_major(state_ml_conv[0]), ((ns, 0), (0, 0)))
    q1, k1, v1, gc, gr, sg, gsk, tail = _ml_in(hs, g1, wi, halo_c, *consts, tm=tms, g_seq=ns,
                                               tiles_per_seq=1, halo_steps=conv_buf + 1)
    extra = ML_SAMPLE_PAD - ts
    pad16 = lambda t: jnp.pad(_seq_major(t, ns), ((0, 0), (0, extra), (0, 0)))
    gc = jnp.pad(_seq_major(gc, ns), ((0, 0), (0, extra), (0, 0)), mode="edge")
    gc = jnp.where((jnp.arange(ML_SAMPLE_PAD)[:, None] >= ts) & (jnp.arange(gc.shape[-1]) < ML_HEADS), NEG, gc)
    gr = jnp.transpose(gc[:, :, :2 * ML_HEADS], (0, 2, 1))
    m0 = jnp.broadcast_to(state_ml_m[0][:, :, None, None], (ns, ML_HEADS, 1, 128))
    hcs, s_c, s_n, s_m = _ml_cell(pad16(q1), pad16(k1), pad16(v1), gc, gr,
                                  state_ml_C[0], state_ml_n[0][:, :, None, :], m0, chunk=ML_SAMPLE_PAD)
    y_sample = _seq_major(_ml_out(_time_major(hcs[:, :ts]), sg, gsk, gn, wout, hs, gf1, w1, w2, gfin, tms), ns)
    s_conv = _seq_major(tail[0], ns)[:, -conv_buf:][None]

    return (y_prompt, y_sample, p_ak, p_av, p_pool,
            p_c[None], p_n[:, :, 0][None], p_m[:, :, 0, 0][None], p_conv,
            s_ak, s_av, s_pool,
            s_c[None], s_n[:, :, 0][None], s_m[:, :, 0, 0][None], s_conv)
```

```python
import functools

import jax
import jax.numpy as jnp
from jax import lax
from jax.experimental import pallas as pl
from jax.experimental.pallas import tpu as pltpu

F32 = jnp.float32
BF16 = jnp.bfloat16

PAST_LEN = 16384
A_HEADS = 8
HEAD_DIM = 64
A_DILATIONS = (1, 4, 16)
A_STEPS = 128
A_BLK = 128
A_SUPER = A_BLK * max(A_DILATIONS)
ATTN_SCALE = HEAD_DIM ** -0.5
POOL_SIZES = (2, 4, 8, 16)
POOL_HALO = 16
ML_HEADS = 4
ML_CONV = 4
ML_QKV_BLOCK = 4
ML_PROMPT_CHUNK = 256
ML_SAMPLE_PAD = 16
ML_HEADS_PER_STEP = 2
ML_NCOLS = 128
ML_SLAB = 128
FFN_CHUNK = 1024
HEADWISE_CHUNK = 256
RMS_EPS = 1e-6
LN_EPS = 1e-5
NEG = -1e30
VMEM_LIMIT = 56 * 1024 * 1024


def _cparams(*sem):
    return pltpu.CompilerParams(dimension_semantics=sem, vmem_limit_bytes=VMEM_LIMIT)


def _const_spec(shape):
    nd = len(shape)
    return pl.BlockSpec(shape, lambda *_: (0,) * nd, pipeline_mode=pl.Buffered(1))


def _rms(x, g):
    return x * lax.rsqrt(jnp.mean(x * x, axis=-1, keepdims=True) + RMS_EPS) * g


def _log_sigmoid(x):
    return jnp.minimum(x, 0.0) - jnp.log(1.0 + jnp.exp(-jnp.abs(x)))


def _ab_in_kernel(x_ref, g_ref, w_ref, q_ref, kf_ref, vf_ref, kb_ref, vb_ref, u_ref):
    aw = q_ref.shape[-1]
    xn = _rms(x_ref[...], g_ref[...]).astype(BF16)
    p = jnp.dot(xn, w_ref[...], preferred_element_type=F32)
    q_ref[...] = (p[:, :aw] * ATTN_SCALE).astype(BF16)
    k = p[:, aw:2 * aw]
    v = p[:, 2 * aw:3 * aw]
    kf_ref[...] = k
    vf_ref[...] = v
    kb_ref[...] = k.astype(BF16)
    vb_ref[...] = v.astype(BF16)
    u_ref[...] = p[:, 3 * aw:]


def _ab_in(x, g, w, tm):
    m, d = x.shape
    aw = A_HEADS * HEAD_DIM
    bw = w.shape[1] - 3 * aw
    row = lambda n: pl.BlockSpec((tm, n), lambda i: (i, 0))
    return pl.pallas_call(
        _ab_in_kernel,
        grid=(m // tm,),
        in_specs=[row(d), _const_spec((1, d)), _const_spec(w.shape)],
        out_specs=[row(aw), row(aw), row(aw), row(aw), row(aw), row(bw)],
        out_shape=[jax.ShapeDtypeStruct((m, aw), BF16), jax.ShapeDtypeStruct((m, aw), F32),
                   jax.ShapeDtypeStruct((m, aw), F32), jax.ShapeDtypeStruct((m, aw), BF16),
                   jax.ShapeDtypeStruct((m, aw), BF16), jax.ShapeDtypeStruct((m, bw), F32)],
        compiler_params=_cparams("arbitrary"),
        name="ab_in",
    )(x, g, w)


def _attn_kernel(q_ref, kc_ref, kp_ref, vc_ref, vp_ref, a_ref, qs, ks, vs, os_, ms_, ds_, bias):
    sb = pl.program_id(1)
    qs[...] = q_ref[0].astype(F32)
    ks[0:A_SUPER, :] = kp_ref[0].astype(F32)
    ks[A_SUPER:, :] = kc_ref[0].astype(F32)
    vs[0:A_SUPER, :] = vp_ref[0].astype(F32)
    vs[A_SUPER:, :] = vc_ref[0].astype(F32)
    qi = lax.broadcasted_iota(jnp.int32, (A_BLK, 2 * A_BLK), 0)
    ki = lax.broadcasted_iota(jnp.int32, (A_BLK, 2 * A_BLK), 1)
    dist = qi - ki + A_BLK
    band = (dist >= 0) & (dist <= A_STEPS)
    bias[0] = jnp.where(band, 0.0, NEG)
    bias[1] = jnp.where(band & (ki >= A_BLK), 0.0, NEG)
    lane = lax.broadcasted_iota(jnp.int32, (A_BLK, 2 * HEAD_DIM), 1)
    low = lane < HEAD_DIM
    nt = (((1,), (1,)), ((), ()))

    for g, dil in enumerate(A_DILATIONS):
        blocks = A_SUPER // (A_BLK * dil)

        def body(idx, carry, g=g, dil=dil, blocks=blocks):
            r = idx // blocks
            n = idx % blocks
            q0 = n * (A_BLK * dil) + r
            k0 = A_SUPER + (n - 1) * (A_BLK * dil) + r
            if dil == 1:
                q0 = pl.multiple_of(q0, A_BLK)
                k0 = pl.multiple_of(k0, A_BLK)
                rows_q = pl.ds(q0, A_BLK)
                rows_k = pl.ds(k0, 2 * A_BLK)
            else:
                rows_q = pl.ds(q0, A_BLK, stride=dil)
                rows_k = pl.ds(k0, 2 * A_BLK, stride=dil)
            qp = qs[rows_q, :]
            kp = ks[rows_k, :].astype(BF16)
            vp = vs[rows_k, :].astype(BF16)
            mask_bias = bias[((n == 0) & (sb == 0)).astype(jnp.int32)]
            accs, maxs, dens = [], [], []
            for j in range(2):
                qm = jnp.where(low if j == 0 else ~low, qp, 0.0).astype(BF16)
                s = lax.dot_general(qm, kp, nt, preferred_element_type=F32) + mask_bias
                m = jnp.max(s, axis=-1, keepdims=True)
                e = jnp.exp(s - m)
                maxs.append(m)
                dens.append(jnp.sum(e, axis=-1, keepdims=True))
                accs.append(jnp.dot(e.astype(BF16), vp, preferred_element_type=F32))
            os_[g, rows_q, :] = jnp.where(low, accs[0], accs[1])
            ms_[g, rows_q, :] = jnp.where(low, maxs[0], maxs[1])
            ds_[g, rows_q, :] = jnp.where(low, dens[0], dens[1])
            return carry

        lax.fori_loop(0, A_SUPER // A_BLK, body, 0, unroll=True)

    ms = [ms_[g] for g in range(len(A_DILATIONS))]
    mm = functools.reduce(jnp.maximum, ms)
    es = [jnp.exp(m - mm) for m in ms]
    num = sum(e * os_[g] for g, e in enumerate(es))
    den = sum(e * ds_[g] for g, e in enumerate(es))
    a_ref[0] = (num / den).astype(a_ref.dtype)


def _attn(q, k, v):
    b, s, aw = q.shape
    pw = 2 * HEAD_DIM
    assert s % A_SUPER == 0
    cur = pl.BlockSpec((1, A_SUPER, pw), lambda bi, sb, p: (bi, sb, p))
    prev = pl.BlockSpec((1, A_SUPER, pw), lambda bi, sb, p: (bi, jnp.maximum(sb - 1, 0), p))
    nd = len(A_DILATIONS)
    return pl.pallas_call(
        _attn_kernel,
        grid=(b, s // A_SUPER, aw // pw),
        in_specs=[cur, cur, prev, cur, prev],
        out_specs=cur,
        out_shape=jax.ShapeDtypeStruct((b, s, aw), BF16),
        scratch_shapes=[pltpu.VMEM((A_SUPER, pw), F32), pltpu.VMEM((2 * A_SUPER, pw), F32),
                        pltpu.VMEM((2 * A_SUPER, pw), F32), pltpu.VMEM((nd, A_SUPER, pw), F32),
                        pltpu.VMEM((nd, A_SUPER, pw), F32), pltpu.VMEM((nd, A_SUPER, pw), F32),
                        pltpu.VMEM((2, A_BLK, 2 * A_BLK), F32)],
        compiler_params=_cparams("arbitrary", "arbitrary", "arbitrary"),
        name="attn",
    )(q, k, k, v, v)


def _sattn_kernel(q_ref, kn_ref, vn_ref, kt_ref, vt_ref, a_ref, *, t_len):
    nh, rows, _ = q_ref.shape[1:]
    buf = kt_ref.shape[-1]
    nd = len(A_DILATIONS)
    rnd = lambda x: x.astype(BF16).astype(F32)
    stack = lambda parts: jnp.concatenate(parts, axis=0)
    trow = lax.broadcasted_iota(jnp.int32, (rows, 1), 0)
    delta = (buf + lax.broadcasted_iota(jnp.int32, (rows, buf), 0)
             - lax.broadcasted_iota(jnp.int32, (rows, buf), 1))
    bias = stack([jnp.where((delta % dil == 0) & (delta <= A_STEPS * dil), 0.0, NEG) for dil in A_DILATIONS])
    for h in range(nh):
        qh = q_ref[0, h]
        knh = rnd(kn_ref[0, h])
        vnh = rnd(vn_ref[0, h])
        s_all = jnp.dot(qh.astype(BF16), kt_ref[0, h].astype(BF16), preferred_element_type=F32)
        s = stack([s_all] * nd) + bias
        s_n = []
        for tp in range(t_len):
            x = jnp.sum(qh * knh[tp:tp + 1, :], axis=-1, keepdims=True)
            s_n.append(stack([jnp.where((trow >= tp) & ((trow - tp) % dil == 0), x, NEG)
                              for dil in A_DILATIONS]))
        m = functools.reduce(jnp.maximum, s_n, jnp.max(s, axis=-1, keepdims=True))
        p = jnp.exp(s - m)
        p_n = [jnp.exp(x - m) for x in s_n]
        den = jnp.sum(p, axis=-1, keepdims=True) + sum(p_n)
        o = lax.dot_general(p.astype(BF16), vt_ref[0, h].astype(BF16), (((1,), (1,)), ((), ())),
                            preferred_element_type=F32)
        o = (o + sum(rnd(pn) * vnh[tp:tp + 1, :] for tp, pn in enumerate(p_n))) / den
        lse = m + jnp.log(den)
        group = lambda x, g: x[g * rows:(g + 1) * rows]
        mm = functools.reduce(jnp.maximum, [group(lse, g) for g in range(nd)])
        es = [jnp.exp(group(lse, g) - mm) for g in range(nd)]
        a_ref[0, h] = sum(e * group(o, g) for g, e in enumerate(es)) / sum(es)


def _sattn(q, kn, vn, kt, vt, t_len):
    buf = kt.shape[-1]
    assert buf >= A_STEPS * max(A_DILATIONS)
    spec = lambda t: pl.BlockSpec((1,) + t.shape[1:], lambda i: (i, 0, 0, 0))
    return pl.pallas_call(
        functools.partial(_sattn_kernel, t_len=t_len),
        grid=(q.shape[0],),
        in_specs=[spec(t) for t in (q, kn, vn, kt, vt)],
        out_specs=spec(q),
        out_shape=jax.ShapeDtypeStruct(q.shape, F32),
        compiler_params=_cparams("arbitrary"),
        name="sattn",
    )(q, kn, vn, kt, vt)


def _ab_out_kernel(a_ref, u_ref, halo_ref, wp_ref, sc_ref, woa_ref, wob_ref, res_ref, g_ref, w1_ref, w2_ref,
                   h_ref, ext_ref, *, g_seq, tiles_per_seq, start, fresh):
    tm, bw = u_ref.shape
    hrows = POOL_HALO * g_seq
    i = pl.program_id(0)
    a = a_ref[...]
    u = u_ref[...]
    halo = halo_ref[...]
    if fresh:
        halo = jnp.where(i % tiles_per_seq == 0, 0.0, halo)
    ext_ref[0:hrows, :] = halo
    ext_ref[hrows:hrows + tm, :] = u
    t_idx = lax.broadcasted_iota(jnp.int32, (tm, 1), 0) // g_seq
    pos = start + (i % tiles_per_seq) * (tm // g_seq) + t_idx
    gw = bw // len(POOL_SIZES)
    y = jnp.zeros((tm, woa_ref.shape[1]), F32)
    y += jnp.dot(a.astype(BF16), woa_ref[...], preferred_element_type=F32)
    for g, w in enumerate(POOL_SIZES):
        cols = slice(g * gw, (g + 1) * gw)
        acc = u[:, cols]
        for j in range(1, w):
            acc = acc + ext_ref[hrows - j * g_seq:hrows - j * g_seq + tm, cols]
        cnt = jnp.minimum(pos + 1, w).astype(F32)
        pooled = acc / cnt - u[:, cols]
        yg = jnp.dot(pooled.astype(BF16), wp_ref[g], preferred_element_type=F32) * sc_ref[:, cols]
        y += jnp.dot(yg.astype(BF16), wob_ref[cols, :], preferred_element_type=F32)
    h_ref[...] = _ffn_tail(res_ref[...] + y, g_ref, w1_ref, w2_ref, None)


def _ab_out(a, u, halo, halo_spec, wp, scale, wo_a, wo_b, res, g, w1, w2, *, tm, g_seq, tiles_per_seq, start,
            fresh):
    m, bw = u.shape
    d = res.shape[1]
    row = lambda n: pl.BlockSpec((tm, n), lambda i: (i, 0))
    kern = functools.partial(_ab_out_kernel, g_seq=g_seq, tiles_per_seq=tiles_per_seq, start=start, fresh=fresh)
    return pl.pallas_call(
        kern,
        grid=(m // tm,),
        in_specs=[row(a.shape[1]), row(bw), halo_spec, _const_spec(wp.shape), _const_spec(scale.shape),
                  _const_spec(wo_a.shape), _const_spec(wo_b.shape), row(d), _const_spec(g.shape),
                  _const_spec(w1.shape), _const_spec(w2.shape)],
        out_specs=row(d),
        out_shape=jax.ShapeDtypeStruct((m, d), F32),
        scratch_shapes=[pltpu.VMEM((POOL_HALO * g_seq + tm, bw), F32)],
        compiler_params=_cparams("arbitrary"),
        name="ab_out",
    )(a, u, halo, wp, scale, wo_a, wo_b, res, g, w1, w2)


def _ffn_tail(x, g_ref, w1_ref, w2_ref, gf_ref):
    xn = _rms(x, g_ref[...]).astype(BF16)
    acc = jnp.zeros(x.shape, F32)
    for c in range(w1_ref.shape[1] // FFN_CHUNK):
        cols = slice(c * FFN_CHUNK, (c + 1) * FFN_CHUNK)
        h = jnp.maximum(jnp.dot(xn, w1_ref[:, cols], preferred_element_type=F32), 0.0)
        acc += jnp.dot((h * h).astype(BF16), w2_ref[cols, :], preferred_element_type=F32)
    out = x + acc
    if gf_ref is not None:
        out = _rms(out, gf_ref[...])
    return out


def _ml_in_kernel(*refs, g_seq, tiles_per_seq, halo_steps, carry):
    if carry:
        (x_ref, g_ref, win_ref, wc_ref, bc_ref, wq_ref, wk_ref, wv_ref, wg_ref, bg_ref, skip_ref,
         q_ref, k_ref, v_ref, gate_ref, grow_ref, sg_ref, gsk_ref, tail_ref, ext_ref) = refs
    else:
        (x_ref, g_ref, win_ref, halo_ref, wc_ref, bc_ref, wq_ref, wk_ref, wv_ref, wg_ref, bg_ref, skip_ref,
         q_ref, k_ref, v_ref, gate_ref, grow_ref, sg_ref, gsk_ref, tail_ref, ext_ref) = refs
    tm = x_ref.shape[0]
    inner = q_ref.shape[1]
    hd = inner // ML_HEADS
    hrows = halo_steps * g_seq
    i = pl.program_id(0)

    xn = _rms(x_ref[...], g_ref[...]).astype(BF16)
    xm = jnp.dot(xn, win_ref[:, :inner], preferred_element_type=F32)
    og = jnp.dot(xn, win_ref[:, inner:], preferred_element_type=F32)

    if carry:
        @pl.when(i % tiles_per_seq == 0)
        def _():
            ext_ref[0:hrows, :] = jnp.zeros((hrows, inner), F32)
    else:
        ext_ref[0:hrows, :] = halo_ref[...]
    ext_ref[hrows:hrows + tm, :] = xm
    conv = xm * wc_ref[ML_CONV - 1:ML_CONV, :] + bc_ref[...]
    for j in range(ML_CONV - 1):
        off = hrows - (ML_CONV - 1 - j) * g_seq
        conv = conv + ext_ref[off:off + tm, :] * wc_ref[j:j + 1, :]
    tail = ext_ref[tm:tm + hrows, :]
    tail_ref[0] = tail
    if carry:
        ext_ref[0:hrows, :] = tail
    ca = conv * jax.nn.sigmoid(conv)
    sig = jax.nn.sigmoid(og)
    sg_ref[...] = sig.astype(BF16)
    gsk_ref[...] = (skip_ref[...] * ca * sig).astype(BF16)

    ca_b = ca.astype(BF16)
    xm_b = xm.astype(BF16)
    gates = jnp.zeros((tm, wg_ref.shape[1]), F32) + bg_ref[...]
    cw = HEADWISE_CHUNK
    for c in range(inner // cw):
        cols = slice(c * cw, (c + 1) * cw)
        qc = jnp.dot(ca_b[:, cols], wq_ref[c], preferred_element_type=F32)
        kc = jnp.dot(ca_b[:, cols], wk_ref[c], preferred_element_type=F32)
        vc = jnp.dot(xm_b[:, cols], wv_ref[c], preferred_element_type=F32)
        qb, kb, vb = qc.astype(BF16), kc.astype(BF16), vc.astype(BF16)
        q_ref[:, cols] = qb
        k_ref[:, cols] = (kc * hd ** -0.5).astype(BF16)
        v_ref[:, cols] = vb
        gates += jnp.dot(qb, wg_ref[c * cw:(c + 1) * cw, :], preferred_element_type=F32)
        gates += jnp.dot(kb, wg_ref[inner + c * cw:inner + (c + 1) * cw, :], preferred_element_type=F32)
        gates += jnp.dot(vb, wg_ref[2 * inner + c * cw:2 * inner + (c + 1) * cw, :], preferred_element_type=F32)
    lane = lax.broadcasted_iota(jnp.int32, gates.shape, 1)
    row = lax.broadcasted_iota(jnp.int32, gates.shape, 0)
    is_f = (lane >= ML_HEADS) & (lane < 2 * ML_HEADS)
    b = jnp.where(is_f, _log_sigmoid(gates), 0.0)
    shift = g_seq
    while shift < tm:
        b = b + jnp.where(row >= shift, pltpu.roll(b, shift, axis=0), 0.0)
        shift *= 2
    packed = jnp.where(is_f, b, gates)
    gate_ref[...] = packed
    grow_ref[0] = jnp.transpose(packed)[0:2 * ML_HEADS, :]


def _ml_in(x, g, w_in, halo, wc, bc, wq, wk, wv, wg, bg, skip, *, tm, g_seq, tiles_per_seq, halo_steps):
    m, d = x.shape
    inner = wc.shape[1]
    carry = halo is None
    hrows = halo_steps * g_seq
    row = lambda n: pl.BlockSpec((tm, n), lambda i: (i, 0))
    consts = [wc, bc, wq, wk, wv, wg, bg, skip]
    args = [x, g, w_in] + ([] if carry else [halo]) + consts
    specs = ([row(d), _const_spec(g.shape), _const_spec(w_in.shape)]
             + ([] if carry else [_const_spec(halo.shape)]) + [_const_spec(t.shape) for t in consts])
    big = lambda dt: jax.ShapeDtypeStruct((m, inner), dt)
    return pl.pallas_call(
        functools.partial(_ml_in_kernel, g_seq=g_seq, tiles_per_seq=tiles_per_seq, halo_steps=halo_steps,
                          carry=carry),
        grid=(m // tm,),
        in_specs=specs,
        out_specs=[row(inner), row(inner), row(inner), row(wg.shape[1]),
                   pl.BlockSpec((1, 2 * ML_HEADS, tm), lambda i: (i, 0, 0)), row(inner), row(inner),
                   pl.BlockSpec((1, hrows, inner), lambda i: (i, 0, 0))],
        out_shape=[big(BF16), big(BF16), big(BF16), jax.ShapeDtypeStruct((m, wg.shape[1]), F32),
                   jax.ShapeDtypeStruct((m // tm, 2 * ML_HEADS, tm), F32),
                   big(BF16), big(BF16), jax.ShapeDtypeStruct((m // tm, hrows, inner), F32)],
        scratch_shapes=[pltpu.VMEM((hrows + tm, inner), F32)],
        compiler_params=_cparams("arbitrary"),
        name="ml_in",
    )(*args)


def _ml_cell_kernel(q_ref, k_ref, v_ref, gcol_ref, grow_ref, c0_ref, n0_ref, m0_ref,
                    h_ref, co_ref, no_ref, mo_ref, cn_sc, m_sc):
    c = pl.program_id(2)
    nc = pl.num_programs(2)
    ln = q_ref.shape[1]
    hd = q_ref.shape[2] // ML_HEADS_PER_STEP
    s_i = lax.broadcasted_iota(jnp.int32, (ln, ln), 0)
    r_i = lax.broadcasted_iota(jnp.int32, (ln, ln), 1)
    causal = r_i <= s_i
    gcol = gcol_ref[0]
    glane = lax.broadcasted_iota(jnp.int32, gcol.shape, 1)

    @pl.when(c == 0)
    def _():
        for j in range(ML_HEADS_PER_STEP):
            cn_sc[j, :, 0:hd] = c0_ref[0, j]
            cn_sc[j, :, hd:] = jnp.transpose(jnp.broadcast_to(n0_ref[0, j], (ML_NCOLS, hd)))
            m_sc[j] = m0_ref[0, j]

    for j in range(ML_HEADS_PER_STEP):
        h = pl.program_id(1) * ML_HEADS_PER_STEP + j
        cols = slice(j * hd, (j + 1) * hd)
        q = q_ref[0, :, cols]
        ks = k_ref[0, :, cols]
        v = v_ref[0, :, cols]
        i_col = jnp.sum(jnp.where(glane == h, gcol, 0.0), axis=1, keepdims=True)
        b_col = jnp.sum(jnp.where(glane == h + ML_HEADS, gcol, 0.0), axis=1, keepdims=True)
        i_row = grow_ref[0, pl.ds(h, 1), :]
        b_row = grow_ref[0, pl.ds(h + ML_HEADS, 1), :]
        m_old = m_sc[j, :, 0:1]

        logw = jnp.where(causal, b_col - b_row + i_row, NEG)
        inter = b_col + m_old
        mt = jnp.maximum(inter, jnp.max(logw, axis=1, keepdims=True))
        scores = lax.dot_general(q, ks, (((1,), (1,)), ((), ())), preferred_element_type=F32)
        a = jnp.exp(logw - mt) * scores
        si = jnp.exp(inter - mt)
        qcn = jnp.dot(q, cn_sc[j].astype(BF16), preferred_element_type=F32)
        num = si * qcn[:, 0:hd] + jnp.dot(a.astype(BF16), v, preferred_element_type=F32)
        den = si * qcn[:, hd:hd + 1] + jnp.sum(a, axis=1, keepdims=True)
        h_ref[0, :, cols] = (num / jnp.maximum(jnp.abs(den), jnp.exp(-mt))).astype(h_ref.dtype)

        b_last = b_col[ln - 1:ln, :]
        wr = b_last - b_col + i_col
        m_new = jnp.maximum(b_last + m_old, jnp.max(wr, axis=0, keepdims=True))
        wk = (jnp.exp(wr - m_new) * ks.astype(F32)).astype(BF16)
        sc = jnp.exp(b_last + m_old - m_new)
        vaug = jnp.concatenate([v, jnp.ones((ln, ML_NCOLS), v.dtype)], axis=1)
        for r in range(hd // ML_SLAB):
            rows = slice(r * ML_SLAB, (r + 1) * ML_SLAB)
            upd = lax.dot_general(wk[:, rows], vaug, (((0,), (0,)), ((), ())), preferred_element_type=F32)
            cn_sc[j, rows, :] = sc * cn_sc[j, rows, :] + upd
        m_sc[j] = jnp.broadcast_to(m_new, m_sc.shape[1:])

    @pl.when(c == nc - 1)
    def _():
        for j in range(ML_HEADS_PER_STEP):
            co_ref[0, j] = cn_sc[j, :, 0:hd]
            no_ref[0, j] = jnp.transpose(cn_sc[j, :, hd:])[0:1, :]
            mo_ref[0, j] = m_sc[j]


def _ml_cell(q, k, v, gcol, grow, c0, n0, m0, *, chunk):
    n, t, inner = q.shape
    hd = inner // ML_HEADS
    hps = ML_HEADS_PER_STEP
    nc = t // chunk
    tok = pl.BlockSpec((1, chunk, hps * hd), lambda b, h, c: (b, c, h))
    st = lambda r, w: pl.BlockSpec((1, hps, r, w), lambda b, h, c: (b, h, 0, 0))
    lanes = m0.shape[-1]
    return pl.pallas_call(
        _ml_cell_kernel,
        grid=(n, ML_HEADS // hps, nc),
        in_specs=[tok, tok, tok, pl.BlockSpec((1, chunk, gcol.shape[-1]), lambda b, h, c: (b, c, 0)),
                  pl.BlockSpec((1, 2 * ML_HEADS, chunk), lambda b, h, c: (b * nc + c, 0, 0)),
                  st(hd, hd), st(1, hd), st(1, lanes)],
        out_specs=[tok, st(hd, hd), st(1, hd), st(1, lanes)],
        out_shape=[jax.ShapeDtypeStruct((n, t, inner), BF16),
                   jax.ShapeDtypeStruct((n, ML_HEADS, hd, hd), F32),
                   jax.ShapeDtypeStruct((n, ML_HEADS, 1, hd), F32),
                   jax.ShapeDtypeStruct((n, ML_HEADS, 1, lanes), F32)],
        scratch_shapes=[pltpu.VMEM((hps, hd, hd + ML_NCOLS), F32), pltpu.VMEM((hps, 1, lanes), F32)],
        compiler_params=_cparams("arbitrary", "arbitrary", "arbitrary"),
        name="ml_cell",
    )(q, k, v, gcol, grow, c0, n0, m0)


def _ml_out_kernel(hc_ref, sg_ref, gsk_ref, gn_ref, w_ref, res_ref, g_ref, w1_ref, w2_ref, gf_ref, o_ref):
    hd = hc_ref.shape[1] // ML_HEADS
    h = res_ref[...]
    for j in range(ML_HEADS):
        cols = slice(j * hd, (j + 1) * hd)
        hc = hc_ref[:, cols].astype(F32)
        dev = hc - jnp.mean(hc, axis=1, keepdims=True)
        var = jnp.mean(dev * dev, axis=1, keepdims=True)
        hn = dev * lax.rsqrt(var + LN_EPS) * gn_ref[:, cols]
        y = hn * sg_ref[:, cols].astype(F32) + gsk_ref[:, cols].astype(F32)
        h = h + jnp.dot(y.astype(BF16), w_ref[cols, :], preferred_element_type=F32)
    o_ref[...] = _ffn_tail(h, g_ref, w1_ref, w2_ref, gf_ref)


def _ml_out(hc, sg, gsk, gn, w, res, g, w1, w2, gf, tm):
    m, kdim = hc.shape
    d = res.shape[1]
    row = lambda n: pl.BlockSpec((tm, n), lambda i: (i, 0))
    return pl.pallas_call(
        _ml_out_kernel,
        grid=(m // tm,),
        in_specs=[row(kdim), row(kdim), row(kdim), _const_spec(gn.shape), _const_spec(w.shape), row(d),
                  _const_spec(g.shape), _const_spec(w1.shape), _const_spec(w2.shape), _const_spec(gf.shape)],
        out_specs=row(d),
        out_shape=jax.ShapeDtypeStruct((m, d), F32),
        compiler_params=_cparams("arbitrary"),
        name="ml_out",
    )(hc, sg, gsk, gn, w, res, g, w1, w2, gf)


def _headwise_dense(w):
    blk = ML_QKV_BLOCK
    rows = w.astype(BF16).reshape(-1, HEADWISE_CHUNK, blk)
    col = jnp.arange(HEADWISE_CHUNK)
    spread = (col[None, :] % blk == jnp.arange(blk)[:, None]).astype(BF16)
    rep = jnp.einsum("crk,kn->crn", rows, spread, preferred_element_type=F32)
    return jnp.where(col[:, None] // blk == col[None, :] // blk, rep, 0.0).astype(BF16)


def _time_major(t):
    return jnp.swapaxes(t, 0, 1).reshape(-1, t.shape[-1])


def _seq_major(t, n):
    return jnp.swapaxes(t.reshape(-1, n, t.shape[-1]), 0, 1)


def kernel(x_prompt, x_sample, cache_a_k, cache_a_v, state_pool, state_ml_C, state_ml_n, state_ml_m, state_ml_conv, norm_mix, norm_ffn, norm_final, ab_w_in, ab_w_pool, ab_pool_scale, ab_w_out, ml_w_in, ml_w_conv, ml_b_conv, ml_w_q, ml_w_k, ml_w_v, ml_w_i, ml_b_i, ml_w_f, ml_b_f, ml_norm, ml_skip, ml_w_out, ffn_w1, ffn_w2):
    b, s, d = x_prompt.shape
    ns, ts, _ = x_sample.shape
    aw = A_HEADS * HEAD_DIM
    inner = ml_w_conv.shape[-1]
    hd = inner // ML_HEADS
    tm = 512
    row1 = lambda t: t.reshape(1, -1)

    hp = x_prompt.reshape(b * s, d)
    hs = _time_major(x_sample)
    tms = hs.shape[0]

    w_in = ab_w_in[0].astype(BF16)
    wp = ab_w_pool[0].astype(BF16)
    psc = row1(ab_pool_scale[0])
    wo = ab_w_out[0].astype(BF16)
    wo_a, wo_b = wo[:aw], wo[aw:]
    g0 = row1(norm_mix[0])
    ffn0 = (row1(norm_ffn[0]), ffn_w1[0].astype(BF16), ffn_w2[0].astype(BF16))

    q, kf, vf, kb, vb, u = _ab_in(hp, g0, w_in, tm)
    to_seq = lambda t: t.reshape(b, s, aw)
    a_p = _attn(to_seq(q), to_seq(kb), to_seq(vb)).reshape(b * s, aw)
    tps = s // tm
    halo_spec = pl.BlockSpec((POOL_HALO, u.shape[1]), lambda i: (jnp.maximum(i * (tm // POOL_HALO) - 1, 0), 0))
    hp = _ab_out(a_p, u, u, halo_spec, wp, psc, wo_a, wo_b, hp, *ffn0,
                 tm=tm, g_seq=1, tiles_per_seq=tps, start=0, fresh=True)
    a_rows = min(A_STEPS * max(A_DILATIONS), s)
    heads = lambda t: t.reshape(t.shape[0], -1, A_HEADS, HEAD_DIM)
    p_ak = heads(to_seq(kf)[:, s - a_rows:])[None]
    p_av = heads(to_seq(vf)[:, s - a_rows:])[None]
    pool_buf = state_pool.shape[2]
    p_pool = u.reshape(b, s, -1)[:, s - pool_buf:][None]

    qs, kfs, vfs, _, _, us = _ab_in(hs, g0, w_in, tms)
    s_ak = heads(_seq_major(kfs, ns))
    s_av = heads(_seq_major(vfs, ns))
    head_major = lambda t: jnp.pad(jnp.swapaxes(t, 1, 2), ((0, 0), (0, 0), (0, 8 - ts), (0, 0)))
    a_s = _sattn(head_major(heads(_seq_major(qs.astype(F32), ns))), head_major(s_ak), head_major(s_av),
                 jnp.transpose(cache_a_k[0], (0, 2, 3, 1)), jnp.transpose(cache_a_v[0], (0, 2, 3, 1)), ts)
    s_ak, s_av = s_ak[None], s_av[None]
    a_s = jnp.transpose(a_s[:, :, :ts], (2, 0, 1, 3)).reshape(ts * ns, aw)
    halo_s = jnp.pad(_time_major(state_pool[0]), ((ns * (POOL_HALO - pool_buf), 0), (0, 0)))
    hs = _ab_out(a_s, us, halo_s, _const_spec(halo_s.shape), wp, psc, wo_a, wo_b, hs, *ffn0,
                 tm=tms, g_seq=ns, tiles_per_seq=1, start=PAST_LEN, fresh=False)
    s_pool = jnp.concatenate([state_pool[0], _seq_major(us, ns)], axis=1)[:, ts:][None]

    g1 = row1(norm_mix[1])
    wi = ml_w_in[0].astype(BF16)
    wq, wk, wv = (_headwise_dense(t[0]) for t in (ml_w_q, ml_w_k, ml_w_v))
    glanes = 128
    wg = jnp.pad(jnp.concatenate([ml_w_i[0], ml_w_f[0]], axis=1), ((0, 0), (0, glanes - 2 * ML_HEADS))).astype(BF16)
    bg = jnp.pad(jnp.concatenate([ml_b_i[0], ml_b_f[0]]), (0, glanes - 2 * ML_HEADS)).reshape(1, glanes)
    consts = (ml_w_conv[0], row1(ml_b_conv[0]), wq, wk, wv, wg, bg, row1(ml_skip[0]))
    gn = row1(ml_norm[0])
    wout = ml_w_out[0].astype(BF16)
    conv_buf = ML_CONV - 1

    w1 = ffn_w1[1].astype(BF16)
    w2 = ffn_w2[1].astype(BF16)
    gf1 = row1(norm_ffn[1])
    gfin = row1(norm_final)

    tm1 = ML_PROMPT_CHUNK
    halo_p = 8
    q1, k1, v1, gc, gr, sg, gsk, tail = _ml_in(hp, g1, wi, None, *consts, tm=tm1, g_seq=1,
                                               tiles_per_seq=s // tm1, halo_steps=halo_p)
    seq3 = lambda t: t.reshape(b, s, -1)
    zeros_state = (jnp.zeros((b, ML_HEADS, hd, hd), F32), jnp.zeros((b, ML_HEADS, 1, hd), F32),
                   jnp.zeros((b, ML_HEADS, 1, 128), F32))
    hc, p_c, p_n, p_m = _ml_cell(seq3(q1), seq3(k1), seq3(v1), seq3(gc), gr, *zeros_state, chunk=ML_PROMPT_CHUNK)
    y_prompt = _ml_out(hc.reshape(b * s, inner), sg, gsk, gn, wout, hp, gf1, w1, w2, gfin, tm).reshape(b, s, d)
    p_conv = tail.reshape(b, s // tm1, halo_p, inner)[:, -1, halo_p - conv_buf:][None]

    halo_c = jnp.pad(_time_major(state_ml_conv[0]), ((ns, 0), (0, 0)))
    q1, k1, v1, gc, gr, sg, gsk, tail = _ml_in(hs, g1, wi, halo_c, *consts, tm=tms, g_seq=ns,
                                               tiles_per_seq=1, halo_steps=conv_buf + 1)
    extra = ML_SAMPLE_PAD - ts
    pad16 = lambda t: jnp.pad(_seq_major(t, ns), ((0, 0), (0, extra), (0, 0)))
    gc = jnp.pad(_seq_major(gc, ns), ((0, 0), (0, extra), (0, 0)), mode="edge")
    gc = jnp.where((jnp.arange(ML_SAMPLE_PAD)[:, None] >= ts) & (jnp.arange(gc.shape[-1]) < ML_HEADS), NEG, gc)
    gr = jnp.transpose(gc[:, :, :2 * ML_HEADS], (0, 2, 1))
    m0 = jnp.broadcast_to(state_ml_m[0][:, :, None, None], (ns, ML_HEADS, 1, 128))
    hcs, s_c, s_n, s_m = _ml_cell(pad16(q1), pad16(k1), pad16(v1), gc, gr,
                                  state_ml_C[0], state_ml_n[0][:, :, None, :], m0, chunk=ML_SAMPLE_PAD)
    y_sample = _seq_major(_ml_out(_time_major(hcs[:, :ts]), sg, gsk, gn, wout, hs, gf1, w1, w2, gfin, tms), ns)
    s_conv = _seq_major(tail[0], ns)[:, -conv_buf:][None]

    return (y_prompt, y_sample, p_ak, p_av, p_pool,
            p_c[None], p_n[:, :, 0][None], p_m[:, :, 0, 0][None], p_conv,
            s_ak, s_av, s_pool,
            s_c[None], s_n[:, :, 0][None], s_m[:, :, 0, 0][None], s_conv)
```

```python
import functools

import jax
import jax.numpy as jnp
from jax import lax
from jax.experimental import pallas as pl
from jax.experimental.pallas import tpu as pltpu

F32 = jnp.float32
BF16 = jnp.bfloat16

PAST_LEN = 16384
A_HEADS = 8
HEAD_DIM = 64
A_DILATIONS = (1, 4, 16)
A_STEPS = 128
A_BLK = 128
A_SUPER = A_BLK * max(A_DILATIONS)
ATTN_SCALE = HEAD_DIM ** -0.5
POOL_SIZES = (2, 4, 8, 16)
POOL_HALO = 16
ML_HEADS = 4
ML_CONV = 4
ML_QKV_BLOCK = 4
ML_PROMPT_CHUNK = 256
ML_SAMPLE_PAD = 16
ML_HEADS_PER_STEP = 2
ML_NCOLS = 128
ML_SLAB = 128
FFN_CHUNK = 1024
HEADWISE_CHUNK = 256
RMS_EPS = 1e-6
LN_EPS = 1e-5
NEG = -1e30
VMEM_LIMIT = 56 * 1024 * 1024


def _cparams(*sem):
    return pltpu.CompilerParams(dimension_semantics=sem, vmem_limit_bytes=VMEM_LIMIT)


def _const_spec(shape):
    nd = len(shape)
    return pl.BlockSpec(shape, lambda *_: (0,) * nd, pipeline_mode=pl.Buffered(1))


def _rms(x, g):
    return x * lax.rsqrt(jnp.mean(x * x, axis=-1, keepdims=True) + RMS_EPS) * g


def _log_sigmoid(x):
    return jnp.minimum(x, 0.0) - jnp.log(1.0 + jnp.exp(-jnp.abs(x)))


def _ab_in_kernel(x_ref, g_ref, w_ref, q_ref, kf_ref, vf_ref, kb_ref, vb_ref, u_ref):
    aw = q_ref.shape[-1]
    xn = _rms(x_ref[...], g_ref[...]).astype(BF16)
    p = jnp.dot(xn, w_ref[...], preferred_element_type=F32)
    q_ref[...] = (p[:, :aw] * ATTN_SCALE).astype(BF16)
    k = p[:, aw:2 * aw]
    v = p[:, 2 * aw:3 * aw]
    kf_ref[...] = k
    vf_ref[...] = v
    kb_ref[...] = k.astype(BF16)
    vb_ref[...] = v.astype(BF16)
    u_ref[...] = p[:, 3 * aw:]


def _ab_in(x, g, w, tm, tiles_per_seq, keep):
    m, d = x.shape
    aw = A_HEADS * HEAD_DIM
    bw = w.shape[1] - 3 * aw
    row = lambda n: pl.BlockSpec((tm, n), lambda i: (i, 0))
    skip = tiles_per_seq - keep
    kept = pl.BlockSpec((tm, aw), lambda i: ((i // tiles_per_seq) * keep + jnp.maximum(i % tiles_per_seq - skip, 0), 0))
    mk = m // tiles_per_seq * keep
    return pl.pallas_call(
        _ab_in_kernel,
        grid=(m // tm,),
        in_specs=[row(d), _const_spec((1, d)), _const_spec(w.shape)],
        out_specs=[row(aw), kept, kept, row(aw), row(aw), row(bw)],
        out_shape=[jax.ShapeDtypeStruct((m, aw), BF16), jax.ShapeDtypeStruct((mk, aw), F32),
                   jax.ShapeDtypeStruct((mk, aw), F32), jax.ShapeDtypeStruct((m, aw), BF16),
                   jax.ShapeDtypeStruct((m, aw), BF16), jax.ShapeDtypeStruct((m, bw), F32)],
        compiler_params=_cparams("arbitrary"),
        name="ab_in",
    )(x, g, w)


def _attn_kernel(q_ref, kc_ref, kp_ref, vc_ref, vp_ref, a_ref, qs, ks, vs, os_, ms_, ds_, bias):
    sb = pl.program_id(1)
    qs[...] = q_ref[0].astype(F32)
    ks[0:A_SUPER, :] = kp_ref[0].astype(F32)
    ks[A_SUPER:, :] = kc_ref[0].astype(F32)
    vs[0:A_SUPER, :] = vp_ref[0].astype(F32)
    vs[A_SUPER:, :] = vc_ref[0].astype(F32)
    qi = lax.broadcasted_iota(jnp.int32, (A_BLK, 2 * A_BLK), 0)
    ki = lax.broadcasted_iota(jnp.int32, (A_BLK, 2 * A_BLK), 1)
    dist = qi - ki + A_BLK
    band = (dist >= 0) & (dist <= A_STEPS)
    bias[0] = jnp.where(band, 0.0, NEG)
    bias[1] = jnp.where(band & (ki >= A_BLK), 0.0, NEG)
    lane = lax.broadcasted_iota(jnp.int32, (A_BLK, 2 * HEAD_DIM), 1)
    low = lane < HEAD_DIM
    nt = (((1,), (1,)), ((), ()))

    def rows(start, size, dil):
        return pl.ds(start, size) if dil == 1 else pl.ds(start, size, stride=dil)

    def scores(g, dil, r, n):
        rows_q = rows(n * A_BLK * dil + r, A_BLK, dil)
        rows_k = rows(A_SUPER + (n - 1) * A_BLK * dil + r, 2 * A_BLK, dil)
        qp = qs[rows_q, :]
        kp = ks[rows_k, :].astype(BF16)
        vp = vs[rows_k, :].astype(BF16)
        mask_bias = bias[(sb == 0).astype(jnp.int32)] if n == 0 else bias[0]
        s = [lax.dot_general(jnp.where(sel, qp, 0.0).astype(BF16), kp, nt, preferred_element_type=F32) + mask_bias
             for sel in (low, ~low)]
        return g, rows_q, s, vp

    def weights(task):
        g, rows_q, s, vp = task
        maxs = [jnp.max(x, axis=-1, keepdims=True) for x in s]
        es = [jnp.exp(x - m) for x, m in zip(s, maxs)]
        dens = [jnp.sum(e, axis=-1, keepdims=True) for e in es]
        return g, rows_q, [e.astype(BF16) for e in es], maxs, dens, vp

    def values(task):
        g, rows_q, es, maxs, dens, vp = task
        accs = [jnp.dot(e, vp, preferred_element_type=F32) for e in es]
        os_[g, rows_q, :] = jnp.where(low, accs[0], accs[1])
        ms_[g, rows_q, :] = jnp.where(low, maxs[0], maxs[1])
        ds_[g, rows_q, :] = jnp.where(low, dens[0], dens[1])

    tasks = [(g, dil, r, n) for g, dil in enumerate(A_DILATIONS)
             for r in range(dil) for n in range(A_SUPER // (A_BLK * dil))]
    scored = {i: scores(*tasks[i]) for i in range(2)}
    weighted = {0: weights(scored.pop(0))}
    for i in range(len(tasks)):
        if i + 2 < len(tasks):
            scored[i + 2] = scores(*tasks[i + 2])
        if i + 1 < len(tasks):
            weighted[i + 1] = weights(scored.pop(i + 1))
        values(weighted.pop(i))

    ms = [ms_[g] for g in range(len(A_DILATIONS))]
    mm = functools.reduce(jnp.maximum, ms)
    es = [jnp.exp(m - mm) for m in ms]
    num = sum(e * os_[g] for g, e in enumerate(es))
    den = sum(e * ds_[g] for g, e in enumerate(es))
    a_ref[0] = (num / den).astype(a_ref.dtype)


def _attn(q, k, v):
    b, s, aw = q.shape
    pw = 2 * HEAD_DIM
    assert s % A_SUPER == 0
    cur = pl.BlockSpec((1, A_SUPER, pw), lambda bi, sb, p: (bi, sb, p))
    prev = pl.BlockSpec((1, A_SUPER, pw), lambda bi, sb, p: (bi, jnp.maximum(sb - 1, 0), p))
    nd = len(A_DILATIONS)
    return pl.pallas_call(
        _attn_kernel,
        grid=(b, s // A_SUPER, aw // pw),
        in_specs=[cur, cur, prev, cur, prev],
        out_specs=cur,
        out_shape=jax.ShapeDtypeStruct((b, s, aw), BF16),
        scratch_shapes=[pltpu.VMEM((A_SUPER, pw), F32), pltpu.VMEM((2 * A_SUPER, pw), F32),
                        pltpu.VMEM((2 * A_SUPER, pw), F32), pltpu.VMEM((nd, A_SUPER, pw), F32),
                        pltpu.VMEM((nd, A_SUPER, pw), F32), pltpu.VMEM((nd, A_SUPER, pw), F32),
                        pltpu.VMEM((2, A_BLK, 2 * A_BLK), F32)],
        compiler_params=_cparams("arbitrary", "arbitrary", "arbitrary"),
        name="attn",
    )(q, k, k, v, v)


def _sattn_kernel(q_ref, kn_ref, vn_ref, kt_ref, vt_ref, a_ref, *, t_len):
    nh, rows, _ = q_ref.shape[1:]
    buf = kt_ref.shape[-1]
    nd = len(A_DILATIONS)
    rnd = lambda x: x.astype(BF16).astype(F32)
    stack = lambda parts: jnp.concatenate(parts, axis=0)
    trow = lax.broadcasted_iota(jnp.int32, (rows, 1), 0)
    delta = (buf + lax.broadcasted_iota(jnp.int32, (rows, buf), 0)
             - lax.broadcasted_iota(jnp.int32, (rows, buf), 1))
    bias = stack([jnp.where((delta % dil == 0) & (delta <= A_STEPS * dil), 0.0, NEG) for dil in A_DILATIONS])
    for h in range(nh):
        qh = q_ref[0, h]
        knh = rnd(kn_ref[0, h])
        vnh = rnd(vn_ref[0, h])
        s_all = jnp.dot(qh.astype(BF16), kt_ref[0, h].astype(BF16), preferred_element_type=F32)
        s = stack([s_all] * nd) + bias
        s_n = []
        for tp in range(t_len):
            x = jnp.sum(qh * knh[tp:tp + 1, :], axis=-1, keepdims=True)
            s_n.append(stack([jnp.where((trow >= tp) & ((trow - tp) % dil == 0), x, NEG)
                              for dil in A_DILATIONS]))
        m = functools.reduce(jnp.maximum, s_n, jnp.max(s, axis=-1, keepdims=True))
        p = jnp.exp(s - m)
        p_n = [jnp.exp(x - m) for x in s_n]
        den = jnp.sum(p, axis=-1, keepdims=True) + sum(p_n)
        o = lax.dot_general(p.astype(BF16), vt_ref[0, h].astype(BF16), (((1,), (1,)), ((), ())),
                            preferred_element_type=F32)
        o = (o + sum(rnd(pn) * vnh[tp:tp + 1, :] for tp, pn in enumerate(p_n))) / den
        lse = m + jnp.log(den)
        group = lambda x, g: x[g * rows:(g + 1) * rows]
        mm = functools.reduce(jnp.maximum, [group(lse, g) for g in range(nd)])
        es = [jnp.exp(group(lse, g) - mm) for g in range(nd)]
        a_ref[0, h] = sum(e * group(o, g) for g, e in enumerate(es)) / sum(es)


def _sattn(q, kn, vn, kt, vt, t_len):
    buf = kt.shape[-1]
    assert buf >= A_STEPS * max(A_DILATIONS)
    spec = lambda t: pl.BlockSpec((1,) + t.shape[1:], lambda i: (i, 0, 0, 0))
    return pl.pallas_call(
        functools.partial(_sattn_kernel, t_len=t_len),
        grid=(q.shape[0],),
        in_specs=[spec(t) for t in (q, kn, vn, kt, vt)],
        out_specs=spec(q),
        out_shape=jax.ShapeDtypeStruct(q.shape, F32),
        compiler_params=_cparams("arbitrary"),
        name="sattn",
    )(q, kn, vn, kt, vt)


def _ab_out_kernel(a_ref, u_ref, halo_ref, wp_ref, sc_ref, woa_ref, wob_ref, res_ref, g_ref, w1_ref, w2_ref,
                   h_ref, ext_ref, *, g_seq, tiles_per_seq, start, fresh):
    tm, bw = u_ref.shape
    hrows = POOL_HALO * g_seq
    i = pl.program_id(0)
    a = a_ref[...]
    u = u_ref[...]
    halo = halo_ref[...]
    if fresh:
        halo = jnp.where(i % tiles_per_seq == 0, 0.0, halo)
    ext_ref[0:hrows, :] = halo
    ext_ref[hrows:hrows + tm, :] = u
    t_idx = lax.broadcasted_iota(jnp.int32, (tm, 1), 0) // g_seq
    pos = start + (i % tiles_per_seq) * (tm // g_seq) + t_idx
    gw = bw // len(POOL_SIZES)
    y = jnp.zeros((tm, woa_ref.shape[1]), F32)
    y += jnp.dot(a.astype(BF16), woa_ref[...], preferred_element_type=F32)
    for g, w in enumerate(POOL_SIZES):
        cols = slice(g * gw, (g + 1) * gw)
        acc = u[:, cols]
        for j in range(1, w):
            acc = acc + ext_ref[hrows - j * g_seq:hrows - j * g_seq + tm, cols]
        cnt = jnp.minimum(pos + 1, w).astype(F32)
        pooled = acc / cnt - u[:, cols]
        yg = jnp.dot(pooled.astype(BF16), wp_ref[g], preferred_element_type=F32) * sc_ref[:, cols]
        y += jnp.dot(yg.astype(BF16), wob_ref[cols, :], preferred_element_type=F32)
    h_ref[...] = _ffn_tail(res_ref[...] + y, g_ref, w1_ref, w2_ref, None)


def _ab_out(a, u, halo, halo_spec, wp, scale, wo_a, wo_b, res, g, w1, w2, *, tm, g_seq, tiles_per_seq, start,
            fresh):
    m, bw = u.shape
    d = res.shape[1]
    row = lambda n: pl.BlockSpec((tm, n), lambda i: (i, 0))
    kern = functools.partial(_ab_out_kernel, g_seq=g_seq, tiles_per_seq=tiles_per_seq, start=start, fresh=fresh)
    return pl.pallas_call(
        kern,
        grid=(m // tm,),
        in_specs=[row(a.shape[1]), row(bw), halo_spec, _const_spec(wp.shape), _const_spec(scale.shape),
                  _const_spec(wo_a.shape), _const_spec(wo_b.shape), row(d), _const_spec(g.shape),
                  _const_spec(w1.shape), _const_spec(w2.shape)],
        out_specs=row(d),
        out_shape=jax.ShapeDtypeStruct((m, d), F32),
        scratch_shapes=[pltpu.VMEM((POOL_HALO * g_seq + tm, bw), F32)],
        compiler_params=_cparams("arbitrary"),
        name="ab_out",
    )(a, u, halo, wp, scale, wo_a, wo_b, res, g, w1, w2)


def _ffn_tail(x, g_ref, w1_ref, w2_ref, gf_ref):
    xn = _rms(x, g_ref[...]).astype(BF16)
    acc = jnp.zeros(x.shape, F32)
    for c in range(w1_ref.shape[1] // FFN_CHUNK):
        cols = slice(c * FFN_CHUNK, (c + 1) * FFN_CHUNK)
        h = jnp.maximum(jnp.dot(xn, w1_ref[:, cols], preferred_element_type=F32), 0.0)
        acc += jnp.dot((h * h).astype(BF16), w2_ref[cols, :], preferred_element_type=F32)
    out = x + acc
    if gf_ref is not None:
        out = _rms(out, gf_ref[...])
    return out


def _ml_in_kernel(*refs, g_seq, tiles_per_seq, halo_steps, carry):
    if carry:
        (x_ref, g_ref, win_ref, wc_ref, bc_ref, wq_ref, wk_ref, wv_ref, wg_ref, bg_ref, skip_ref,
         q_ref, k_ref, v_ref, gate_ref, grow_ref, sg_ref, gsk_ref, tail_ref, ext_ref) = refs
    else:
        (x_ref, g_ref, win_ref, halo_ref, wc_ref, bc_ref, wq_ref, wk_ref, wv_ref, wg_ref, bg_ref, skip_ref,
         q_ref, k_ref, v_ref, gate_ref, grow_ref, sg_ref, gsk_ref, tail_ref, ext_ref) = refs
    tm = x_ref.shape[0]
    inner = q_ref.shape[1]
    hd = inner // ML_HEADS
    hrows = halo_steps * g_seq
    i = pl.program_id(0)

    if carry:
        @pl.when(i % tiles_per_seq == 0)
        def _():
            ext_ref[0:hrows, :] = jnp.zeros((hrows, inner), F32)
    else:
        ext_ref[0:hrows, :] = halo_ref[...]

    xn = _rms(x_ref[...], g_ref[...]).astype(BF16)
    gates = jnp.zeros((tm, wg_ref.shape[1]), F32) + bg_ref[...]
    cw = HEADWISE_CHUNK
    def up_proj(c):
        return (jnp.dot(xn, win_ref[:, c * cw:(c + 1) * cw], preferred_element_type=F32),
                jnp.dot(xn, win_ref[:, inner + c * cw:inner + (c + 1) * cw], preferred_element_type=F32))

    def activate(c, xm, og):
        cols = slice(c * cw, (c + 1) * cw)
        ext_ref[hrows:hrows + tm, cols] = xm
        conv = xm * wc_ref[ML_CONV - 1:ML_CONV, cols] + bc_ref[:, cols]
        for j in range(ML_CONV - 1):
            off = hrows - (ML_CONV - 1 - j) * g_seq
            conv = conv + ext_ref[off:off + tm, cols] * wc_ref[j:j + 1, cols]
        tail = ext_ref[tm:tm + hrows, cols]
        tail_ref[0, :, cols] = tail
        if carry:
            ext_ref[0:hrows, cols] = tail
        ca = conv * jax.nn.sigmoid(conv)
        sig = jax.nn.sigmoid(og)
        sg_ref[:, cols] = sig.astype(BF16)
        gsk_ref[:, cols] = (skip_ref[:, cols] * ca * sig).astype(BF16)
        return ca.astype(BF16), xm.astype(BF16)

    nchunks = inner // cw
    ups = {c: up_proj(c) for c in range(min(2, nchunks))}
    acts = {0: activate(0, *ups.pop(0))}
    for c in range(nchunks):
        cols = slice(c * cw, (c + 1) * cw)
        if c + 2 < nchunks:
            ups[c + 2] = up_proj(c + 2)
        if c + 1 < nchunks:
            acts[c + 1] = activate(c + 1, *ups.pop(c + 1))
        ca_b, xm_b = acts.pop(c)
        qc = jnp.dot(ca_b, wq_ref[c], preferred_element_type=F32)
        kc = jnp.dot(ca_b, wk_ref[c], preferred_element_type=F32)
        vc = jnp.dot(xm_b, wv_ref[c], preferred_element_type=F32)
        qb, kb, vb = qc.astype(BF16), kc.astype(BF16), vc.astype(BF16)
        q_ref[:, cols] = qb
        k_ref[:, cols] = (kc * hd ** -0.5).astype(BF16)
        v_ref[:, cols] = vb
        gates += jnp.dot(qb, wg_ref[c * cw:(c + 1) * cw, :], preferred_element_type=F32)
        gates += jnp.dot(kb, wg_ref[inner + c * cw:inner + (c + 1) * cw, :], preferred_element_type=F32)
        gates += jnp.dot(vb, wg_ref[2 * inner + c * cw:2 * inner + (c + 1) * cw, :], preferred_element_type=F32)
    lane = lax.broadcasted_iota(jnp.int32, gates.shape, 1)
    row = lax.broadcasted_iota(jnp.int32, gates.shape, 0)
    is_f = (lane >= ML_HEADS) & (lane < 2 * ML_HEADS)
    b = jnp.where(is_f, _log_sigmoid(gates), 0.0)
    shift = g_seq
    while shift < tm:
        b = b + jnp.where(row >= shift, pltpu.roll(b, shift, axis=0), 0.0)
        shift *= 2
    packed = jnp.where(is_f, b, gates)
    gate_ref[...] = packed
    grow_ref[0] = jnp.transpose(packed)[0:2 * ML_HEADS, :]


def _ml_in(x, g, w_in, halo, wc, bc, wq, wk, wv, wg, bg, skip, *, tm, g_seq, tiles_per_seq, halo_steps):
    m, d = x.shape
    inner = wc.shape[1]
    carry = halo is None
    hrows = halo_steps * g_seq
    row = lambda n: pl.BlockSpec((tm, n), lambda i: (i, 0))
    consts = [wc, bc, wq, wk, wv, wg, bg, skip]
    args = [x, g, w_in] + ([] if carry else [halo]) + consts
    specs = ([row(d), _const_spec(g.shape), _const_spec(w_in.shape)]
             + ([] if carry else [_const_spec(halo.shape)]) + [_const_spec(t.shape) for t in consts])
    big = lambda dt: jax.ShapeDtypeStruct((m, inner), dt)
    return pl.pallas_call(
        functools.partial(_ml_in_kernel, g_seq=g_seq, tiles_per_seq=tiles_per_seq, halo_steps=halo_steps,
                          carry=carry),
        grid=(m // tm,),
        in_specs=specs,
        out_specs=[row(inner), row(inner), row(inner), row(wg.shape[1]),
                   pl.BlockSpec((1, 2 * ML_HEADS, tm), lambda i: (i, 0, 0)), row(inner), row(inner),
                   pl.BlockSpec((1, hrows, inner), lambda i: (i, 0, 0))],
        out_shape=[big(BF16), big(BF16), big(BF16), jax.ShapeDtypeStruct((m, wg.shape[1]), F32),
                   jax.ShapeDtypeStruct((m // tm, 2 * ML_HEADS, tm), F32),
                   big(BF16), big(BF16), jax.ShapeDtypeStruct((m // tm, hrows, inner), F32)],
        scratch_shapes=[pltpu.VMEM((hrows + tm, inner), F32)],
        compiler_params=_cparams("arbitrary"),
        name="ml_in",
    )(*args)


def _ml_cell_kernel(q_ref, k_ref, v_ref, gcol_ref, grow_ref, c0_ref, n0_ref, m0_ref,
                    h_ref, co_ref, no_ref, mo_ref, cn_sc, m_sc):
    c = pl.program_id(2)
    nc = pl.num_programs(2)
    ln = q_ref.shape[1]
    hd = q_ref.shape[2] // ML_HEADS_PER_STEP
    s_i = lax.broadcasted_iota(jnp.int32, (ln, ln), 0)
    r_i = lax.broadcasted_iota(jnp.int32, (ln, ln), 1)
    causal = r_i <= s_i
    gcol = gcol_ref[0]
    glane = lax.broadcasted_iota(jnp.int32, gcol.shape, 1)

    @pl.when(c == 0)
    def _():
        for j in range(ML_HEADS_PER_STEP):
            cn_sc[j, :, 0:hd] = c0_ref[0, j]
            cn_sc[j, :, hd:] = jnp.transpose(jnp.broadcast_to(n0_ref[0, j], (ML_NCOLS, hd)))
            m_sc[j] = m0_ref[0, j]

    for j in range(ML_HEADS_PER_STEP):
        h = pl.program_id(1) * ML_HEADS_PER_STEP + j
        cols = slice(j * hd, (j + 1) * hd)
        q = q_ref[0, :, cols]
        ks = k_ref[0, :, cols]
        v = v_ref[0, :, cols]
        i_col = jnp.sum(jnp.where(glane == h, gcol, 0.0), axis=1, keepdims=True)
        b_col = jnp.sum(jnp.where(glane == h + ML_HEADS, gcol, 0.0), axis=1, keepdims=True)
        i_row = grow_ref[0, pl.ds(h, 1), :]
        b_row = grow_ref[0, pl.ds(h + ML_HEADS, 1), :]
        m_old = m_sc[j, :, 0:1]

        logw = jnp.where(causal, b_col - b_row + i_row, NEG)
        inter = b_col + m_old
        mt = jnp.maximum(inter, jnp.max(logw, axis=1, keepdims=True))
        scores = lax.dot_general(q, ks, (((1,), (1,)), ((), ())), preferred_element_type=F32)
        a = jnp.exp(logw - mt) * scores
        si = jnp.exp(inter - mt)
        qcn = jnp.dot(q, cn_sc[j].astype(BF16), preferred_element_type=F32)
        num = si * qcn[:, 0:hd] + jnp.dot(a.astype(BF16), v, preferred_element_type=F32)
        den = si * qcn[:, hd:hd + 1] + jnp.sum(a, axis=1, keepdims=True)
        h_ref[0, :, cols] = (num / jnp.maximum(jnp.abs(den), jnp.exp(-mt))).astype(h_ref.dtype)

        b_last = b_col[ln - 1:ln, :]
        wr = b_last - b_col + i_col
        m_new = jnp.maximum(b_last + m_old, jnp.max(wr, axis=0, keepdims=True))
        wk = (jnp.exp(wr - m_new) * ks.astype(F32)).astype(BF16)
        sc = jnp.exp(b_last + m_old - m_new)
        vaug = jnp.concatenate([v, jnp.ones((ln, ML_NCOLS), v.dtype)], axis=1)
        for r in range(hd // ML_SLAB):
            rows = slice(r * ML_SLAB, (r + 1) * ML_SLAB)
            upd = lax.dot_general(wk[:, rows], vaug, (((0,), (0,)), ((), ())), preferred_element_type=F32)
            cn_sc[j, rows, :] = sc * cn_sc[j, rows, :] + upd
        m_sc[j] = jnp.broadcast_to(m_new, m_sc.shape[1:])

    @pl.when(c == nc - 1)
    def _():
        for j in range(ML_HEADS_PER_STEP):
            co_ref[0, j] = cn_sc[j, :, 0:hd]
            no_ref[0, j] = jnp.transpose(cn_sc[j, :, hd:])[0:1, :]
            mo_ref[0, j] = m_sc[j]


def _ml_cell(q, k, v, gcol, grow, c0, n0, m0, *, chunk):
    n, t, inner = q.shape
    hd = inner // ML_HEADS
    hps = ML_HEADS_PER_STEP
    nc = t // chunk
    tok = pl.BlockSpec((1, chunk, hps * hd), lambda b, h, c: (b, c, h))
    st = lambda r, w: pl.BlockSpec((1, hps, r, w), lambda b, h, c: (b, h, 0, 0))
    lanes = m0.shape[-1]
    return pl.pallas_call(
        _ml_cell_kernel,
        grid=(n, ML_HEADS // hps, nc),
        in_specs=[tok, tok, tok, pl.BlockSpec((1, chunk, gcol.shape[-1]), lambda b, h, c: (b, c, 0)),
                  pl.BlockSpec((1, 2 * ML_HEADS, chunk), lambda b, h, c: (b * nc + c, 0, 0)),
                  st(hd, hd), st(1, hd), st(1, lanes)],
        out_specs=[tok, st(hd, hd), st(1, hd), st(1, lanes)],
        out_shape=[jax.ShapeDtypeStruct((n, t, inner), BF16),
                   jax.ShapeDtypeStruct((n, ML_HEADS, hd, hd), F32),
                   jax.ShapeDtypeStruct((n, ML_HEADS, 1, hd), F32),
                   jax.ShapeDtypeStruct((n, ML_HEADS, 1, lanes), F32)],
        scratch_shapes=[pltpu.VMEM((hps, hd, hd + ML_NCOLS), F32), pltpu.VMEM((hps, 1, lanes), F32)],
        compiler_params=_cparams("arbitrary", "arbitrary", "arbitrary"),
        name="ml_cell",
    )(q, k, v, gcol, grow, c0, n0, m0)


def _ml_out_kernel(hc_ref, sg_ref, gsk_ref, gn_ref, w_ref, res_ref, g_ref, w1_ref, w2_ref, gf_ref, o_ref):
    hd = hc_ref.shape[1] // ML_HEADS
    h = res_ref[...]
    for j in range(ML_HEADS):
        cols = slice(j * hd, (j + 1) * hd)
        hc = hc_ref[:, cols].astype(F32)
        dev = hc - jnp.mean(hc, axis=1, keepdims=True)
        var = jnp.mean(dev * dev, axis=1, keepdims=True)
        hn = dev * lax.rsqrt(var + LN_EPS) * gn_ref[:, cols]
        y = hn * sg_ref[:, cols].astype(F32) + gsk_ref[:, cols].astype(F32)
        h = h + jnp.dot(y.astype(BF16), w_ref[cols, :], preferred_element_type=F32)
    o_ref[...] = _ffn_tail(h, g_ref, w1_ref, w2_ref, gf_ref)


def _ml_out(hc, sg, gsk, gn, w, res, g, w1, w2, gf, tm):
    m, kdim = hc.shape
    d = res.shape[1]
    row = lambda n: pl.BlockSpec((tm, n), lambda i: (i, 0))
    return pl.pallas_call(
        _ml_out_kernel,
        grid=(m // tm,),
        in_specs=[row(kdim), row(kdim), row(kdim), _const_spec(gn.shape), _const_spec(w.shape), row(d),
                  _const_spec(g.shape), _const_spec(w1.shape), _const_spec(w2.shape), _const_spec(gf.shape)],
        out_specs=row(d),
        out_shape=jax.ShapeDtypeStruct((m, d), F32),
        compiler_params=_cparams("arbitrary"),
        name="ml_out",
    )(hc, sg, gsk, gn, w, res, g, w1, w2, gf)


def _headwise_dense(w):
    blk = ML_QKV_BLOCK
    rows = w.astype(BF16).reshape(-1, HEADWISE_CHUNK, blk)
    col = jnp.arange(HEADWISE_CHUNK)
    spread = (col[None, :] % blk == jnp.arange(blk)[:, None]).astype(BF16)
    rep = jnp.einsum("crk,kn->crn", rows, spread, preferred_element_type=F32)
    return jnp.where(col[:, None] // blk == col[None, :] // blk, rep, 0.0).astype(BF16)


def _time_major(t):
    return jnp.swapaxes(t, 0, 1).reshape(-1, t.shape[-1])


def _seq_major(t, n):
    return jnp.swapaxes(t.reshape(-1, n, t.shape[-1]), 0, 1)


def kernel(x_prompt, x_sample, cache_a_k, cache_a_v, state_pool, state_ml_C, state_ml_n, state_ml_m, state_ml_conv, norm_mix, norm_ffn, norm_final, ab_w_in, ab_w_pool, ab_pool_scale, ab_w_out, ml_w_in, ml_w_conv, ml_b_conv, ml_w_q, ml_w_k, ml_w_v, ml_w_i, ml_b_i, ml_w_f, ml_b_f, ml_norm, ml_skip, ml_w_out, ffn_w1, ffn_w2):
    b, s, d = x_prompt.shape
    ns, ts, _ = x_sample.shape
    aw = A_HEADS * HEAD_DIM
    inner = ml_w_conv.shape[-1]
    hd = inner // ML_HEADS
    tm = 512
    row1 = lambda t: t.reshape(1, -1)

    hp = x_prompt.reshape(b * s, d)
    hs = _time_major(x_sample)
    tms = hs.shape[0]

    w_in = ab_w_in[0].astype(BF16)
    wp = ab_w_pool[0].astype(BF16)
    psc = row1(ab_pool_scale[0])
    wo = ab_w_out[0].astype(BF16)
    wo_a, wo_b = wo[:aw], wo[aw:]
    g0 = row1(norm_mix[0])
    ffn0 = (row1(norm_ffn[0]), ffn_w1[0].astype(BF16), ffn_w2[0].astype(BF16))

    tps = s // tm
    a_rows = min(A_STEPS * max(A_DILATIONS), s)
    assert a_rows % tm == 0
    q, kf, vf, kb, vb, u = _ab_in(hp, g0, w_in, tm, tps, a_rows // tm)
    to_seq = lambda t: t.reshape(b, s, aw)
    a_p = _attn(to_seq(q), to_seq(kb), to_seq(vb)).reshape(b * s, aw)
    halo_spec = pl.BlockSpec((POOL_HALO, u.shape[1]), lambda i: (jnp.maximum(i * (tm // POOL_HALO) - 1, 0), 0))
    hp = _ab_out(a_p, u, u, halo_spec, wp, psc, wo_a, wo_b, hp, *ffn0,
                 tm=tm, g_seq=1, tiles_per_seq=tps, start=0, fresh=True)
    heads = lambda t: t.reshape(t.shape[0], -1, A_HEADS, HEAD_DIM)
    p_ak = heads(kf.reshape(b, a_rows, aw))[None]
    p_av = heads(vf.reshape(b, a_rows, aw))[None]
    pool_buf = state_pool.shape[2]
    p_pool = u.reshape(b, s, -1)[:, s - pool_buf:][None]

    qs, kfs, vfs, _, _, us = _ab_in(hs, g0, w_in, tms, 1, 1)
    s_ak = heads(_seq_major(kfs, ns))
    s_av = heads(_seq_major(vfs, ns))
    head_major = lambda t: jnp.pad(jnp.swapaxes(t, 1, 2), ((0, 0), (0, 0), (0, 8 - ts), (0, 0)))
    a_s = _sattn(head_major(heads(_seq_major(qs.astype(F32), ns))), head_major(s_ak), head_major(s_av),
                 jnp.transpose(cache_a_k[0], (0, 2, 3, 1)), jnp.transpose(cache_a_v[0], (0, 2, 3, 1)), ts)
    s_ak, s_av = s_ak[None], s_av[None]
    a_s = jnp.transpose(a_s[:, :, :ts], (2, 0, 1, 3)).reshape(ts * ns, aw)
    halo_s = jnp.pad(_time_major(state_pool[0]), ((ns * (POOL_HALO - pool_buf), 0), (0, 0)))
    hs = _ab_out(a_s, us, halo_s, _const_spec(halo_s.shape), wp, psc, wo_a, wo_b, hs, *ffn0,
                 tm=tms, g_seq=ns, tiles_per_seq=1, start=PAST_LEN, fresh=False)
    s_pool = jnp.concatenate([state_pool[0], _seq_major(us, ns)], axis=1)[:, ts:][None]

    g1 = row1(norm_mix[1])
    wi = ml_w_in[0].astype(BF16)
    wq, wk, wv = (_headwise_dense(t[0]) for t in (ml_w_q, ml_w_k, ml_w_v))
    glanes = 128
    wg = jnp.pad(jnp.concatenate([ml_w_i[0], ml_w_f[0]], axis=1), ((0, 0), (0, glanes - 2 * ML_HEADS))).astype(BF16)
    bg = jnp.pad(jnp.concatenate([ml_b_i[0], ml_b_f[0]]), (0, glanes - 2 * ML_HEADS)).reshape(1, glanes)
    consts = (ml_w_conv[0], row1(ml_b_conv[0]), wq, wk, wv, wg, bg, row1(ml_skip[0]))
    gn = row1(ml_norm[0])
    wout = ml_w_out[0].astype(BF16)
    conv_buf = ML_CONV - 1

    w1 = ffn_w1[1].astype(BF16)
    w2 = ffn_w2[1].astype(BF16)
    gf1 = row1(norm_ffn[1])
    gfin = row1(norm_final)

    tm1 = ML_PROMPT_CHUNK
    halo_p = 8
    q1, k1, v1, gc, gr, sg, gsk, tail = _ml_in(hp, g1, wi, None, *consts, tm=tm1, g_seq=1,
                                               tiles_per_seq=s // tm1, halo_steps=halo_p)
    seq3 = lambda t: t.reshape(b, s, -1)
    zeros_state = (jnp.zeros((b, ML_HEADS, hd, hd), F32), jnp.zeros((b, ML_HEADS, 1, hd), F32),
                   jnp.zeros((b, ML_HEADS, 1, 128), F32))
    hc, p_c, p_n, p_m = _ml_cell(seq3(q1), seq3(k1), seq3(v1), seq3(gc), gr, *zeros_state, chunk=ML_PROMPT_CHUNK)
    y_prompt = _ml_out(hc.reshape(b * s, inner), sg, gsk, gn, wout, hp, gf1, w1, w2, gfin, tm).reshape(b, s, d)
    p_conv = tail.reshape(b, s // tm1, halo_p, inner)[:, -1, halo_p - conv_buf:][None]

    halo_c = jnp.pad(_time_major(state_ml_conv[0]), ((ns, 0), (0, 0)))
    q1, k1, v1, gc, gr, sg, gsk, tail = _ml_in(hs, g1, wi, halo_c, *consts, tm=tms, g_seq=ns,
                                               tiles_per_seq=1, halo_steps=conv_buf + 1)
    extra = ML_SAMPLE_PAD - ts
    pad16 = lambda t: jnp.pad(_seq_major(t, ns), ((0, 0), (0, extra), (0, 0)))
    gc = jnp.pad(_seq_major(gc, ns), ((0, 0), (0, extra), (0, 0)), mode="edge")
    gc = jnp.where((jnp.arange(ML_SAMPLE_PAD)[:, None] >= ts) & (jnp.arange(gc.shape[-1]) < ML_HEADS), NEG, gc)
    gr = jnp.transpose(gc[:, :, :2 * ML_HEADS], (0, 2, 1))
    m0 = jnp.broadcast_to(state_ml_m[0][:, :, None, None], (ns, ML_HEADS, 1, 128))
    hcs, s_c, s_n, s_m = _ml_cell(pad16(q1), pad16(k1), pad16(v1), gc, gr,
                                  state_ml_C[0], state_ml_n[0][:, :, None, :], m0, chunk=ML_SAMPLE_PAD)
    y_sample = _seq_major(_ml_out(_time_major(hcs[:, :ts]), sg, gsk, gn, wout, hs, gf1, w1, w2, gfin, tms), ns)
    s_conv = _seq_major(tail[0], ns)[:, -conv_buf:][None]

    return (y_prompt, y_sample, p_ak, p_av, p_pool,
            p_c[None], p_n[:, :, 0][None], p_m[:, :, 0, 0][None], p_conv,
            s_ak, s_av, s_pool,
            s_c[None], s_n[:, :, 0][None], s_m[:, :, 0, 0][None], s_conv)
```

```python
import functools

import jax
import jax.numpy as jnp
from jax import lax
from jax.experimental import pallas as pl
from jax.experimental.pallas import tpu as pltpu

F32 = jnp.float32
BF16 = jnp.bfloat16

PAST_LEN = 16384
A_HEADS = 8
HEAD_DIM = 64
A_DILATIONS = (1, 4, 16)
A_STEPS = 128
A_BLK = 128
A_QROWS = 128
A_SUPER = A_BLK * max(A_DILATIONS)
ATTN_SCALE = HEAD_DIM ** -0.5
POOL_SIZES = (2, 4, 8, 16)
POOL_HALO = 16
ML_HEADS = 4
ML_CONV = 4
ML_QKV_BLOCK = 4
ML_PROMPT_CHUNK = 256
ML_SAMPLE_PAD = 16
ML_HEADS_PER_STEP = 4
ML_NCOLS = 128
ML_SLAB = 128
FFN_CHUNK = 1024
HEADWISE_CHUNK = 256
RMS_EPS = 1e-6
LN_EPS = 1e-5
NEG = -1e30
VMEM_LIMIT = 56 * 1024 * 1024


def _cparams(*sem):
    return pltpu.CompilerParams(dimension_semantics=sem, vmem_limit_bytes=VMEM_LIMIT)


def _const_spec(shape):
    nd = len(shape)
    return pl.BlockSpec(shape, lambda *_: (0,) * nd, pipeline_mode=pl.Buffered(1))


def _rms(x, g):
    return x * lax.rsqrt(jnp.mean(x * x, axis=-1, keepdims=True) + RMS_EPS) * g


def _log_sigmoid(x):
    return jnp.minimum(x, 0.0) - jnp.log(1.0 + jnp.exp(-jnp.abs(x)))


def _ab_in_kernel(x_ref, g_ref, w_ref, q_ref, kf_ref, vf_ref, kb_ref, vb_ref, u_ref):
    aw = q_ref.shape[-1]
    xn = _rms(x_ref[...], g_ref[...]).astype(BF16)
    p = jnp.dot(xn, w_ref[...], preferred_element_type=F32)
    q_ref[...] = (p[:, :aw] * ATTN_SCALE).astype(BF16)
    k = p[:, aw:2 * aw]
    v = p[:, 2 * aw:3 * aw]
    kf_ref[...] = k
    vf_ref[...] = v
    kb_ref[...] = k.astype(BF16)
    vb_ref[...] = v.astype(BF16)
    u_ref[...] = p[:, 3 * aw:]


def _ab_in(x, g, w, tm, tiles_per_seq, keep):
    m, d = x.shape
    aw = A_HEADS * HEAD_DIM
    bw = w.shape[1] - 3 * aw
    row = lambda n: pl.BlockSpec((tm, n), lambda i: (i, 0))
    skip = tiles_per_seq - keep
    kept = pl.BlockSpec((tm, aw), lambda i: ((i // tiles_per_seq) * keep + jnp.maximum(i % tiles_per_seq - skip, 0), 0))
    mk = m // tiles_per_seq * keep
    return pl.pallas_call(
        _ab_in_kernel,
        grid=(m // tm,),
        in_specs=[row(d), _const_spec((1, d)), _const_spec(w.shape)],
        out_specs=[row(aw), kept, kept, row(aw), row(aw), row(bw)],
        out_shape=[jax.ShapeDtypeStruct((m, aw), BF16), jax.ShapeDtypeStruct((mk, aw), F32),
                   jax.ShapeDtypeStruct((mk, aw), F32), jax.ShapeDtypeStruct((m, aw), BF16),
                   jax.ShapeDtypeStruct((m, aw), BF16), jax.ShapeDtypeStruct((m, bw), F32)],
        compiler_params=_cparams("arbitrary"),
        name="ab_in",
    )(x, g, w)


def _attn_kernel(q_ref, kc_ref, kp_ref, vc_ref, vp_ref, a_ref, qs, ks, vs, os_, ms_, ds_, bias):
    sb = pl.program_id(1)
    qs[...] = q_ref[0].astype(F32)
    ks[0:A_SUPER, :] = kp_ref[0].astype(F32)
    ks[A_SUPER:, :] = kc_ref[0].astype(F32)
    vs[0:A_SUPER, :] = vp_ref[0].astype(F32)
    vs[A_SUPER:, :] = vc_ref[0].astype(F32)
    qi = lax.broadcasted_iota(jnp.int32, (A_BLK, 2 * A_BLK), 0)
    ki = lax.broadcasted_iota(jnp.int32, (A_BLK, 2 * A_BLK), 1)
    dist = qi - ki + A_BLK
    band = (dist >= 0) & (dist <= A_STEPS)
    bias[0] = jnp.where(band, 0.0, NEG)
    bias[1] = jnp.where(band & (ki >= A_BLK), 0.0, NEG)
    lane = lax.broadcasted_iota(jnp.int32, (A_QROWS, 2 * HEAD_DIM), 1)
    low = lane < HEAD_DIM
    nt = (((1,), (1,)), ((), ()))

    def rows(start, size, dil):
        return pl.ds(start, size) if dil == 1 else pl.ds(start, size, stride=dil)

    def scores(g, dil, r, n, part):
        rows_q = rows((n * A_BLK + part * A_QROWS) * dil + r, A_QROWS, dil)
        rows_k = rows(A_SUPER + (n - 1) * A_BLK * dil + r, 2 * A_BLK, dil)
        qp = qs[rows_q, :]
        kp = ks[rows_k, :].astype(BF16)
        vp = vs[rows_k, :].astype(BF16)
        which = (sb == 0).astype(jnp.int32) if n == 0 else 0
        mask_bias = bias[which, part * A_QROWS:(part + 1) * A_QROWS, :]
        q2 = jnp.concatenate([jnp.where(sel, qp, 0.0).astype(BF16) for sel in (low, ~low)], axis=0)
        s2 = lax.dot_general(q2, kp, nt, preferred_element_type=F32)
        s = [s2[j * A_QROWS:(j + 1) * A_QROWS] + mask_bias for j in range(2)]
        return g, rows_q, s, vp

    def weights(task):
        g, rows_q, s, vp = task
        maxs = [jnp.max(x, axis=-1, keepdims=True) for x in s]
        es = [jnp.exp(x - m) for x, m in zip(s, maxs)]
        dens = [jnp.sum(e, axis=-1, keepdims=True) for e in es]
        return g, rows_q, [e.astype(BF16) for e in es], maxs, dens, vp

    def values(task):
        g, rows_q, es, maxs, dens, vp = task
        acc2 = jnp.dot(jnp.concatenate(es, axis=0), vp, preferred_element_type=F32)
        accs = [acc2[j * A_QROWS:(j + 1) * A_QROWS] for j in range(2)]
        os_[g, rows_q, :] = jnp.where(low, accs[0], accs[1])
        ms_[g, rows_q, :] = jnp.where(low, maxs[0], maxs[1])
        ds_[g, rows_q, :] = jnp.where(low, dens[0], dens[1])

    tasks = [(g, dil, r, n, part) for g, dil in enumerate(A_DILATIONS)
             for r in range(dil) for n in range(A_SUPER // (A_BLK * dil)) for part in range(A_BLK // A_QROWS)]
    scored = {i: scores(*tasks[i]) for i in range(2)}
    weighted = {0: weights(scored.pop(0))}
    for i in range(len(tasks)):
        if i + 2 < len(tasks):
            scored[i + 2] = scores(*tasks[i + 2])
        if i + 1 < len(tasks):
            weighted[i + 1] = weights(scored.pop(i + 1))
        values(weighted.pop(i))

    ms = [ms_[g] for g in range(len(A_DILATIONS))]
    mm = functools.reduce(jnp.maximum, ms)
    es = [jnp.exp(m - mm) for m in ms]
    num = sum(e * os_[g] for g, e in enumerate(es))
    den = sum(e * ds_[g] for g, e in enumerate(es))
    a_ref[0] = (num / den).astype(a_ref.dtype)


def _attn(q, k, v):
    b, s, aw = q.shape
    pw = 2 * HEAD_DIM
    assert s % A_SUPER == 0
    cur = pl.BlockSpec((1, A_SUPER, pw), lambda bi, sb, p: (bi, sb, p))
    prev = pl.BlockSpec((1, A_SUPER, pw), lambda bi, sb, p: (bi, jnp.maximum(sb - 1, 0), p))
    nd = len(A_DILATIONS)
    return pl.pallas_call(
        _attn_kernel,
        grid=(b, s // A_SUPER, aw // pw),
        in_specs=[cur, cur, prev, cur, prev],
        out_specs=cur,
        out_shape=jax.ShapeDtypeStruct((b, s, aw), BF16),
        scratch_shapes=[pltpu.VMEM((A_SUPER, pw), F32), pltpu.VMEM((2 * A_SUPER, pw), F32),
                        pltpu.VMEM((2 * A_SUPER, pw), F32), pltpu.VMEM((nd, A_SUPER, pw), F32),
                        pltpu.VMEM((nd, A_SUPER, pw), F32), pltpu.VMEM((nd, A_SUPER, pw), F32),
                        pltpu.VMEM((2, A_BLK, 2 * A_BLK), F32)],
        compiler_params=_cparams("arbitrary", "arbitrary", "arbitrary"),
        name="attn",
    )(q, k, k, v, v)


def _sattn_kernel(q_ref, kn_ref, vn_ref, kt_ref, vt_ref, a_ref, *, t_len):
    nh, rows, _ = q_ref.shape[1:]
    buf = kt_ref.shape[-1]
    nd = len(A_DILATIONS)
    rnd = lambda x: x.astype(BF16).astype(F32)
    stack = lambda parts: jnp.concatenate(parts, axis=0)
    trow = lax.broadcasted_iota(jnp.int32, (rows, 1), 0)
    delta = (buf + lax.broadcasted_iota(jnp.int32, (rows, buf), 0)
             - lax.broadcasted_iota(jnp.int32, (rows, buf), 1))
    bias = stack([jnp.where((delta % dil == 0) & (delta <= A_STEPS * dil), 0.0, NEG) for dil in A_DILATIONS])
    for h in range(nh):
        qh = q_ref[0, h]
        knh = rnd(kn_ref[0, h])
        vnh = rnd(vn_ref[0, h])
        s_all = jnp.dot(qh.astype(BF16), kt_ref[0, h].astype(BF16), preferred_element_type=F32)
        s = stack([s_all] * nd) + bias
        s_n = []
        for tp in range(t_len):
            x = jnp.sum(qh * knh[tp:tp + 1, :], axis=-1, keepdims=True)
            s_n.append(stack([jnp.where((trow >= tp) & ((trow - tp) % dil == 0), x, NEG)
                              for dil in A_DILATIONS]))
        m = functools.reduce(jnp.maximum, s_n, jnp.max(s, axis=-1, keepdims=True))
        p = jnp.exp(s - m)
        p_n = [jnp.exp(x - m) for x in s_n]
        den = jnp.sum(p, axis=-1, keepdims=True) + sum(p_n)
        o = lax.dot_general(p.astype(BF16), vt_ref[0, h].astype(BF16), (((1,), (1,)), ((), ())),
                            preferred_element_type=F32)
        o = (o + sum(rnd(pn) * vnh[tp:tp + 1, :] for tp, pn in enumerate(p_n))) / den
        lse = m + jnp.log(den)
        group = lambda x, g: x[g * rows:(g + 1) * rows]
        mm = functools.reduce(jnp.maximum, [group(lse, g) for g in range(nd)])
        es = [jnp.exp(group(lse, g) - mm) for g in range(nd)]
        a_ref[0, h] = sum(e * group(o, g) for g, e in enumerate(es)) / sum(es)


def _sattn(q, kn, vn, kt, vt, t_len):
    buf = kt.shape[-1]
    assert buf >= A_STEPS * max(A_DILATIONS)
    spec = lambda t: pl.BlockSpec((1,) + t.shape[1:], lambda i: (i, 0, 0, 0))
    return pl.pallas_call(
        functools.partial(_sattn_kernel, t_len=t_len),
        grid=(q.shape[0],),
        in_specs=[spec(t) for t in (q, kn, vn, kt, vt)],
        out_specs=spec(q),
        out_shape=jax.ShapeDtypeStruct(q.shape, F32),
        compiler_params=_cparams("arbitrary"),
        name="sattn",
    )(q, kn, vn, kt, vt)


def _ab_out_kernel(a_ref, u_ref, halo_ref, wp_ref, sc_ref, woa_ref, wob_ref, res_ref, g_ref, w1_ref, w2_ref,
                   h_ref, ext_ref, *, g_seq, tiles_per_seq, start, fresh):
    tm, bw = u_ref.shape
    hrows = POOL_HALO * g_seq
    i = pl.program_id(0)
    a = a_ref[...]
    u = u_ref[...]
    halo = halo_ref[...]
    if fresh:
        halo = jnp.where(i % tiles_per_seq == 0, 0.0, halo)
    ext_ref[0:hrows, :] = halo
    ext_ref[hrows:hrows + tm, :] = u
    t_idx = lax.broadcasted_iota(jnp.int32, (tm, 1), 0) // g_seq
    pos = start + (i % tiles_per_seq) * (tm // g_seq) + t_idx
    gw = bw // len(POOL_SIZES)
    y = jnp.zeros((tm, woa_ref.shape[1]), F32)
    y += jnp.dot(a.astype(BF16), woa_ref[...], preferred_element_type=F32)
    for g, w in enumerate(POOL_SIZES):
        cols = slice(g * gw, (g + 1) * gw)
        acc = u[:, cols]
        for j in range(1, w):
            acc = acc + ext_ref[hrows - j * g_seq:hrows - j * g_seq + tm, cols]
        cnt = jnp.minimum(pos + 1, w).astype(F32)
        pooled = acc / cnt - u[:, cols]
        yg = jnp.dot(pooled.astype(BF16), wp_ref[g], preferred_element_type=F32) * sc_ref[:, cols]
        y += jnp.dot(yg.astype(BF16), wob_ref[cols, :], preferred_element_type=F32)
    h_ref[...] = _ffn_tail(res_ref[...] + y, g_ref, w1_ref, w2_ref, None)


def _ab_out(a, u, halo, halo_spec, wp, scale, wo_a, wo_b, res, g, w1, w2, *, tm, g_seq, tiles_per_seq, start,
            fresh):
    m, bw = u.shape
    d = res.shape[1]
    row = lambda n: pl.BlockSpec((tm, n), lambda i: (i, 0))
    kern = functools.partial(_ab_out_kernel, g_seq=g_seq, tiles_per_seq=tiles_per_seq, start=start, fresh=fresh)
    return pl.pallas_call(
        kern,
        grid=(m // tm,),
        in_specs=[row(a.shape[1]), row(bw), halo_spec, _const_spec(wp.shape), _const_spec(scale.shape),
                  _const_spec(wo_a.shape), _const_spec(wo_b.shape), row(d), _const_spec(g.shape),
                  _const_spec(w1.shape), _const_spec(w2.shape)],
        out_specs=row(d),
        out_shape=jax.ShapeDtypeStruct((m, d), F32),
        scratch_shapes=[pltpu.VMEM((POOL_HALO * g_seq + tm, bw), F32)],
        compiler_params=_cparams("arbitrary"),
        name="ab_out",
    )(a, u, halo, wp, scale, wo_a, wo_b, res, g, w1, w2)


def _ffn_tail(x, g_ref, w1_ref, w2_ref, gf_ref):
    xn = _rms(x, g_ref[...]).astype(BF16)
    acc = jnp.zeros(x.shape, F32)
    for c in range(w1_ref.shape[1] // FFN_CHUNK):
        cols = slice(c * FFN_CHUNK, (c + 1) * FFN_CHUNK)
        h = jnp.maximum(jnp.dot(xn, w1_ref[:, cols], preferred_element_type=F32), 0.0)
        acc += jnp.dot((h * h).astype(BF16), w2_ref[cols, :], preferred_element_type=F32)
    out = x + acc
    if gf_ref is not None:
        out = _rms(out, gf_ref[...])
    return out


def _ml_in_kernel(*refs, g_seq, tiles_per_seq, halo_steps, carry):
    if carry:
        (x_ref, g_ref, win_ref, wc_ref, bc_ref, wq_ref, wk_ref, wv_ref, wg_ref, bg_ref, skip_ref,
         q_ref, k_ref, v_ref, gate_ref, grow_ref, sg_ref, gsk_ref, tail_ref, ext_ref) = refs
    else:
        (x_ref, g_ref, win_ref, halo_ref, wc_ref, bc_ref, wq_ref, wk_ref, wv_ref, wg_ref, bg_ref, skip_ref,
         q_ref, k_ref, v_ref, gate_ref, grow_ref, sg_ref, gsk_ref, tail_ref, ext_ref) = refs
    tm = x_ref.shape[0]
    inner = q_ref.shape[1]
    hd = inner // ML_HEADS
    hrows = halo_steps * g_seq
    i = pl.program_id(0)

    if carry:
        @pl.when(i % tiles_per_seq == 0)
        def _():
            ext_ref[0:hrows, :] = jnp.zeros((hrows, inner), F32)
    else:
        ext_ref[0:hrows, :] = halo_ref[...]

    xn = _rms(x_ref[...], g_ref[...]).astype(BF16)
    gates = jnp.zeros((tm, wg_ref.shape[1]), F32) + bg_ref[...]
    cw = HEADWISE_CHUNK
    def up_proj(c):
        return (jnp.dot(xn, win_ref[:, c * cw:(c + 1) * cw], preferred_element_type=F32),
                jnp.dot(xn, win_ref[:, inner + c * cw:inner + (c + 1) * cw], preferred_element_type=F32))

    def activate(c, xm, og):
        cols = slice(c * cw, (c + 1) * cw)
        ext_ref[hrows:hrows + tm, cols] = xm
        conv = xm * wc_ref[ML_CONV - 1:ML_CONV, cols] + bc_ref[:, cols]
        for j in range(ML_CONV - 1):
            off = hrows - (ML_CONV - 1 - j) * g_seq
            conv = conv + ext_ref[off:off + tm, cols] * wc_ref[j:j + 1, cols]
        tail = ext_ref[tm:tm + hrows, cols]
        tail_ref[0, :, cols] = tail
        if carry:
            ext_ref[0:hrows, cols] = tail
        ca = conv * jax.nn.sigmoid(conv)
        sig = jax.nn.sigmoid(og)
        sg_ref[:, cols] = sig.astype(BF16)
        gsk_ref[:, cols] = (skip_ref[:, cols] * ca * sig).astype(BF16)
        return ca.astype(BF16), xm.astype(BF16)

    nchunks = inner // cw
    ups = {c: up_proj(c) for c in range(min(2, nchunks))}
    acts = {0: activate(0, *ups.pop(0))}
    for c in range(nchunks):
        cols = slice(c * cw, (c + 1) * cw)
        if c + 2 < nchunks:
            ups[c + 2] = up_proj(c + 2)
        if c + 1 < nchunks:
            acts[c + 1] = activate(c + 1, *ups.pop(c + 1))
        ca_b, xm_b = acts.pop(c)
        qc = jnp.dot(ca_b, wq_ref[c], preferred_element_type=F32)
        kc = jnp.dot(ca_b, wk_ref[c], preferred_element_type=F32)
        vc = jnp.dot(xm_b, wv_ref[c], preferred_element_type=F32)
        qb, kb, vb = qc.astype(BF16), kc.astype(BF16), vc.astype(BF16)
        q_ref[:, cols] = qb
        k_ref[:, cols] = (kc * hd ** -0.5).astype(BF16)
        v_ref[:, cols] = vb
        gates += jnp.dot(qb, wg_ref[c * cw:(c + 1) * cw, :], preferred_element_type=F32)
        gates += jnp.dot(kb, wg_ref[inner + c * cw:inner + (c + 1) * cw, :], preferred_element_type=F32)
        gates += jnp.dot(vb, wg_ref[2 * inner + c * cw:2 * inner + (c + 1) * cw, :], preferred_element_type=F32)
    lane = lax.broadcasted_iota(jnp.int32, gates.shape, 1)
    row = lax.broadcasted_iota(jnp.int32, gates.shape, 0)
    is_f = (lane >= ML_HEADS) & (lane < 2 * ML_HEADS)
    b = jnp.where(is_f, _log_sigmoid(gates), 0.0)
    shift = g_seq
    while shift < tm:
        b = b + jnp.where(row >= shift, pltpu.roll(b, shift, axis=0), 0.0)
        shift *= 2
    packed = jnp.where(is_f, b, gates)
    gate_ref[...] = packed
    grow_ref[0] = jnp.transpose(packed)[0:2 * ML_HEADS, :]


def _ml_in(x, g, w_in, halo, wc, bc, wq, wk, wv, wg, bg, skip, *, tm, g_seq, tiles_per_seq, halo_steps):
    m, d = x.shape
    inner = wc.shape[1]
    carry = halo is None
    hrows = halo_steps * g_seq
    row = lambda n: pl.BlockSpec((tm, n), lambda i: (i, 0))
    consts = [wc, bc, wq, wk, wv, wg, bg, skip]
    args = [x, g, w_in] + ([] if carry else [halo]) + consts
    specs = ([row(d), _const_spec(g.shape), _const_spec(w_in.shape)]
             + ([] if carry else [_const_spec(halo.shape)]) + [_const_spec(t.shape) for t in consts])
    big = lambda dt: jax.ShapeDtypeStruct((m, inner), dt)
    return pl.pallas_call(
        functools.partial(_ml_in_kernel, g_seq=g_seq, tiles_per_seq=tiles_per_seq, halo_steps=halo_steps,
                          carry=carry),
        grid=(m // tm,),
        in_specs=specs,
        out_specs=[row(inner), row(inner), row(inner), row(wg.shape[1]),
                   pl.BlockSpec((1, 2 * ML_HEADS, tm), lambda i: (i, 0, 0)), row(inner), row(inner),
                   pl.BlockSpec((1, hrows, inner), lambda i: (i, 0, 0))],
        out_shape=[big(BF16), big(BF16), big(BF16), jax.ShapeDtypeStruct((m, wg.shape[1]), F32),
                   jax.ShapeDtypeStruct((m // tm, 2 * ML_HEADS, tm), F32),
                   big(BF16), big(BF16), jax.ShapeDtypeStruct((m // tm, hrows, inner), F32)],
        scratch_shapes=[pltpu.VMEM((hrows + tm, inner), F32)],
        compiler_params=_cparams("arbitrary"),
        name="ml_in",
    )(*args)


def _ml_cell_kernel(q_ref, k_ref, v_ref, gcol_ref, grow_ref, c0_ref, n0_ref, m0_ref,
                    h_ref, co_ref, no_ref, mo_ref, cn_sc, m_sc):
    c = pl.program_id(2)
    nc = pl.num_programs(2)
    ln = q_ref.shape[1]
    hd = q_ref.shape[2] // ML_HEADS_PER_STEP
    s_i = lax.broadcasted_iota(jnp.int32, (ln, ln), 0)
    r_i = lax.broadcasted_iota(jnp.int32, (ln, ln), 1)
    causal = r_i <= s_i
    gcol = gcol_ref[0]
    glane = lax.broadcasted_iota(jnp.int32, gcol.shape, 1)

    @pl.when(c == 0)
    def _():
        for j in range(ML_HEADS_PER_STEP):
            cn_sc[j, :, 0:hd] = c0_ref[0, j]
            cn_sc[j, :, hd:] = jnp.transpose(jnp.broadcast_to(n0_ref[0, j], (ML_NCOLS, hd)))
            m_sc[j] = m0_ref[0, j]

    def read_stage(j):
        h = pl.program_id(1) * ML_HEADS_PER_STEP + j
        cols = slice(j * hd, (j + 1) * hd)
        q = q_ref[0, :, cols]
        ks = k_ref[0, :, cols]
        i_col = jnp.sum(jnp.where(glane == h, gcol, 0.0), axis=1, keepdims=True)
        b_col = jnp.sum(jnp.where(glane == h + ML_HEADS, gcol, 0.0), axis=1, keepdims=True)
        i_row = grow_ref[0, pl.ds(h, 1), :]
        b_row = grow_ref[0, pl.ds(h + ML_HEADS, 1), :]
        m_old = m_sc[j, :, 0:1]
        logw = jnp.where(causal, b_col - b_row + i_row, NEG)
        inter = b_col + m_old
        mt = jnp.maximum(inter, jnp.max(logw, axis=1, keepdims=True))
        scores = lax.dot_general(q, ks, (((1,), (1,)), ((), ())), preferred_element_type=F32)
        qcn = jnp.dot(q, cn_sc[j].astype(BF16), preferred_element_type=F32)
        return j, cols, ks, i_col, b_col, m_old, logw, inter, mt, scores, qcn

    def output_stage(st):
        j, cols, ks, i_col, b_col, m_old, logw, inter, mt, scores, qcn = st
        a = jnp.exp(logw - mt) * scores
        si = jnp.exp(inter - mt)
        num = si * qcn[:, 0:hd] + jnp.dot(a.astype(BF16), v_ref[0, :, cols], preferred_element_type=F32)
        den = si * qcn[:, hd:hd + 1] + jnp.sum(a, axis=1, keepdims=True)
        h_ref[0, :, cols] = (num / jnp.maximum(jnp.abs(den), jnp.exp(-mt))).astype(h_ref.dtype)

    def state_stage(st):
        j, cols, ks, i_col, b_col, m_old = st[:6]
        b_last = b_col[ln - 1:ln, :]
        wr = b_last - b_col + i_col
        m_new = jnp.maximum(b_last + m_old, jnp.max(wr, axis=0, keepdims=True))
        wk = (jnp.exp(wr - m_new) * ks.astype(F32)).astype(BF16)
        sc = jnp.exp(b_last + m_old - m_new)
        v = v_ref[0, :, cols]
        vaug = jnp.concatenate([v, jnp.ones((ln, ML_NCOLS), v.dtype)], axis=1)
        for r in range(hd // ML_SLAB):
            rows = slice(r * ML_SLAB, (r + 1) * ML_SLAB)
            upd = lax.dot_general(wk[:, rows], vaug, (((0,), (0,)), ((), ())), preferred_element_type=F32)
            cn_sc[j, rows, :] = sc * cn_sc[j, rows, :] + upd
        m_sc[j] = jnp.broadcast_to(m_new, m_sc.shape[1:])

    for j in range(ML_HEADS_PER_STEP):
        st = read_stage(j)
        output_stage(st)
        state_stage(st)

    @pl.when(c == nc - 1)
    def _():
        for j in range(ML_HEADS_PER_STEP):
            co_ref[0, j] = cn_sc[j, :, 0:hd]
            no_ref[0, j] = jnp.transpose(cn_sc[j, :, hd:])[0:1, :]
            mo_ref[0, j] = m_sc[j]


def _ml_cell(q, k, v, gcol, grow, c0, n0, m0, *, chunk):
    n, t, inner = q.shape
    hd = inner // ML_HEADS
    hps = ML_HEADS_PER_STEP
    nc = t // chunk
    tok = pl.BlockSpec((1, chunk, hps * hd), lambda b, h, c: (b, c, h))
    st = lambda r, w: pl.BlockSpec((1, hps, r, w), lambda b, h, c: (b, h, 0, 0))
    lanes = m0.shape[-1]
    return pl.pallas_call(
        _ml_cell_kernel,
        grid=(n, ML_HEADS // hps, nc),
        in_specs=[tok, tok, tok, pl.BlockSpec((1, chunk, gcol.shape[-1]), lambda b, h, c: (b, c, 0)),
                  pl.BlockSpec((1, 2 * ML_HEADS, chunk), lambda b, h, c: (b * nc + c, 0, 0)),
                  st(hd, hd), st(1, hd), st(1, lanes)],
        out_specs=[tok, st(hd, hd), st(1, hd), st(1, lanes)],
        out_shape=[jax.ShapeDtypeStruct((n, t, inner), BF16),
                   jax.ShapeDtypeStruct((n, ML_HEADS, hd, hd), F32),
                   jax.ShapeDtypeStruct((n, ML_HEADS, 1, hd), F32),
                   jax.ShapeDtypeStruct((n, ML_HEADS, 1, lanes), F32)],
        scratch_shapes=[pltpu.VMEM((hps, hd, hd + ML_NCOLS), F32), pltpu.VMEM((hps, 1, lanes), F32)],
        compiler_params=_cparams("arbitrary", "arbitrary", "arbitrary"),
        name="ml_cell",
    )(q, k, v, gcol, grow, c0, n0, m0)


def _ml_out_kernel(hc_ref, sg_ref, gsk_ref, gn_ref, w_ref, res_ref, g_ref, w1_ref, w2_ref, gf_ref, o_ref):
    hd = hc_ref.shape[1] // ML_HEADS
    h = res_ref[...]
    for j in range(ML_HEADS):
        cols = slice(j * hd, (j + 1) * hd)
        hc = hc_ref[:, cols].astype(F32)
        dev = hc - jnp.mean(hc, axis=1, keepdims=True)
        var = jnp.mean(dev * dev, axis=1, keepdims=True)
        hn = dev * lax.rsqrt(var + LN_EPS) * gn_ref[:, cols]
        y = hn * sg_ref[:, cols].astype(F32) + gsk_ref[:, cols].astype(F32)
        h = h + jnp.dot(y.astype(BF16), w_ref[cols, :], preferred_element_type=F32)
    o_ref[...] = _ffn_tail(h, g_ref, w1_ref, w2_ref, gf_ref)


def _ml_out(hc, sg, gsk, gn, w, res, g, w1, w2, gf, tm):
    m, kdim = hc.shape
    d = res.shape[1]
    row = lambda n: pl.BlockSpec((tm, n), lambda i: (i, 0))
    return pl.pallas_call(
        _ml_out_kernel,
        grid=(m // tm,),
        in_specs=[row(kdim), row(kdim), row(kdim), _const_spec(gn.shape), _const_spec(w.shape), row(d),
                  _const_spec(g.shape), _const_spec(w1.shape), _const_spec(w2.shape), _const_spec(gf.shape)],
        out_specs=row(d),
        out_shape=jax.ShapeDtypeStruct((m, d), F32),
        compiler_params=_cparams("arbitrary"),
        name="ml_out",
    )(hc, sg, gsk, gn, w, res, g, w1, w2, gf)


def _headwise_dense(w):
    blk = ML_QKV_BLOCK
    rows = w.astype(BF16).reshape(-1, HEADWISE_CHUNK, blk)
    col = jnp.arange(HEADWISE_CHUNK)
    spread = (col[None, :] % blk == jnp.arange(blk)[:, None]).astype(BF16)
    rep = jnp.einsum("crk,kn->crn", rows, spread, preferred_element_type=F32)
    return jnp.where(col[:, None] // blk == col[None, :] // blk, rep, 0.0).astype(BF16)


def _time_major(t):
    return jnp.swapaxes(t, 0, 1).reshape(-1, t.shape[-1])


def _seq_major(t, n):
    return jnp.swapaxes(t.reshape(-1, n, t.shape[-1]), 0, 1)


def kernel(x_prompt, x_sample, cache_a_k, cache_a_v, state_pool, state_ml_C, state_ml_n, state_ml_m, state_ml_conv, norm_mix, norm_ffn, norm_final, ab_w_in, ab_w_pool, ab_pool_scale, ab_w_out, ml_w_in, ml_w_conv, ml_b_conv, ml_w_q, ml_w_k, ml_w_v, ml_w_i, ml_b_i, ml_w_f, ml_b_f, ml_norm, ml_skip, ml_w_out, ffn_w1, ffn_w2):
    b, s, d = x_prompt.shape
    ns, ts, _ = x_sample.shape
    aw = A_HEADS * HEAD_DIM
    inner = ml_w_conv.shape[-1]
    hd = inner // ML_HEADS
    tm = 512
    row1 = lambda t: t.reshape(1, -1)

    hp = x_prompt.reshape(b * s, d)
    hs = _time_major(x_sample)
    tms = hs.shape[0]

    w_in = ab_w_in[0].astype(BF16)
    wp = ab_w_pool[0].astype(BF16)
    psc = row1(ab_pool_scale[0])
    wo = ab_w_out[0].astype(BF16)
    wo_a, wo_b = wo[:aw], wo[aw:]
    g0 = row1(norm_mix[0])
    ffn0 = (row1(norm_ffn[0]), ffn_w1[0].astype(BF16), ffn_w2[0].astype(BF16))

    tps = s // tm
    a_rows = min(A_STEPS * max(A_DILATIONS), s)
    assert a_rows % tm == 0
    q, kf, vf, kb, vb, u = _ab_in(hp, g0, w_in, tm, tps, a_rows // tm)
    to_seq = lambda t: t.reshape(b, s, aw)
    a_p = _attn(to_seq(q), to_seq(kb), to_seq(vb)).reshape(b * s, aw)
    halo_spec = pl.BlockSpec((POOL_HALO, u.shape[1]), lambda i: (jnp.maximum(i * (tm // POOL_HALO) - 1, 0), 0))
    hp = _ab_out(a_p, u, u, halo_spec, wp, psc, wo_a, wo_b, hp, *ffn0,
                 tm=tm, g_seq=1, tiles_per_seq=tps, start=0, fresh=True)
    heads = lambda t: t.reshape(t.shape[0], -1, A_HEADS, HEAD_DIM)
    p_ak = heads(kf.reshape(b, a_rows, aw))[None]
    p_av = heads(vf.reshape(b, a_rows, aw))[None]
    pool_buf = state_pool.shape[2]
    p_pool = u.reshape(b, s, -1)[:, s - pool_buf:][None]

    qs, kfs, vfs, _, _, us = _ab_in(hs, g0, w_in, tms, 1, 1)
    s_ak = heads(_seq_major(kfs, ns))
    s_av = heads(_seq_major(vfs, ns))
    head_major = lambda t: jnp.pad(jnp.swapaxes(t, 1, 2), ((0, 0), (0, 0), (0, 8 - ts), (0, 0)))
    a_s = _sattn(head_major(heads(_seq_major(qs.astype(F32), ns))), head_major(s_ak), head_major(s_av),
                 jnp.transpose(cache_a_k[0], (0, 2, 3, 1)), jnp.transpose(cache_a_v[0], (0, 2, 3, 1)), ts)
    s_ak, s_av = s_ak[None], s_av[None]
    a_s = jnp.transpose(a_s[:, :, :ts], (2, 0, 1, 3)).reshape(ts * ns, aw)
    halo_s = jnp.pad(_time_major(state_pool[0]), ((ns * (POOL_HALO - pool_buf), 0), (0, 0)))
    hs = _ab_out(a_s, us, halo_s, _const_spec(halo_s.shape), wp, psc, wo_a, wo_b, hs, *ffn0,
                 tm=tms, g_seq=ns, tiles_per_seq=1, start=PAST_LEN, fresh=False)
    s_pool = jnp.concatenate([state_pool[0], _seq_major(us, ns)], axis=1)[:, ts:][None]

    g1 = row1(norm_mix[1])
    wi = ml_w_in[0].astype(BF16)
    wq, wk, wv = (_headwise_dense(t[0]) for t in (ml_w_q, ml_w_k, ml_w_v))
    glanes = 128
    wg = jnp.pad(jnp.concatenate([ml_w_i[0], ml_w_f[0]], axis=1), ((0, 0), (0, glanes - 2 * ML_HEADS))).astype(BF16)
    bg = jnp.pad(jnp.concatenate([ml_b_i[0], ml_b_f[0]]), (0, glanes - 2 * ML_HEADS)).reshape(1, glanes)
    consts = (ml_w_conv[0], row1(ml_b_conv[0]), wq, wk, wv, wg, bg, row1(ml_skip[0]))
    gn = row1(ml_norm[0])
    wout = ml_w_out[0].astype(BF16)
    conv_buf = ML_CONV - 1

    w1 = ffn_w1[1].astype(BF16)
    w2 = ffn_w2[1].astype(BF16)
    gf1 = row1(norm_ffn[1])
    gfin = row1(norm_final)

    tm1 = ML_PROMPT_CHUNK
    halo_p = 8
    q1, k1, v1, gc, gr, sg, gsk, tail = _ml_in(hp, g1, wi, None, *consts, tm=tm1, g_seq=1,
                                               tiles_per_seq=s // tm1, halo_steps=halo_p)
    seq3 = lambda t: t.reshape(b, s, -1)
    zeros_state = (jnp.zeros((b, ML_HEADS, hd, hd), F32), jnp.zeros((b, ML_HEADS, 1, hd), F32),
                   jnp.zeros((b, ML_HEADS, 1, 128), F32))
    hc, p_c, p_n, p_m = _ml_cell(seq3(q1), seq3(k1), seq3(v1), seq3(gc), gr, *zeros_state, chunk=ML_PROMPT_CHUNK)
    y_prompt = _ml_out(hc.reshape(b * s, inner), sg, gsk, gn, wout, hp, gf1, w1, w2, gfin, tm).reshape(b, s, d)
    p_conv = tail.reshape(b, s // tm1, halo_p, inner)[:, -1, halo_p - conv_buf:][None]

    halo_c = jnp.pad(_time_major(state_ml_conv[0]), ((ns, 0), (0, 0)))
    q1, k1, v1, gc, gr, sg, gsk, tail = _ml_in(hs, g1, wi, halo_c, *consts, tm=tms, g_seq=ns,
                                               tiles_per_seq=1, halo_steps=conv_buf + 1)
    extra = ML_SAMPLE_PAD - ts
    pad16 = lambda t: jnp.pad(_seq_major(t, ns), ((0, 0), (0, extra), (0, 0)))
    gc = jnp.pad(_seq_major(gc, ns), ((0, 0), (0, extra), (0, 0)), mode="edge")
    gc = jnp.where((jnp.arange(ML_SAMPLE_PAD)[:, None] >= ts) & (jnp.arange(gc.shape[-1]) < ML_HEADS), NEG, gc)
    gr = jnp.transpose(gc[:, :, :2 * ML_HEADS], (0, 2, 1))
    m0 = jnp.broadcast_to(state_ml_m[0][:, :, None, None], (ns, ML_HEADS, 1, 128))
    hcs, s_c, s_n, s_m = _ml_cell(pad16(q1), pad16(k1), pad16(v1), gc, gr,
                                  state_ml_C[0], state_ml_n[0][:, :, None, :], m0, chunk=ML_SAMPLE_PAD)
    y_sample = _seq_major(_ml_out(_time_major(hcs[:, :ts]), sg, gsk, gn, wout, hs, gf1, w1, w2, gfin, tms), ns)
    s_conv = _seq_major(tail[0], ns)[:, -conv_buf:][None]

    return (y_prompt, y_sample, p_ak, p_av, p_pool,
            p_c[None], p_n[:, :, 0][None], p_m[:, :, 0, 0][None], p_conv,
            s_ak, s_av, s_pool,
            s_c[None], s_n[:, :, 0][None], s_m[:, :, 0, 0][None], s_conv)
```

```python
import functools

import jax
import jax.numpy as jnp
from jax import lax
from jax.experimental import pallas as pl
from jax.experimental.pallas import tpu as pltpu

F32 = jnp.float32
BF16 = jnp.bfloat16

PAST_LEN = 16384
A_HEADS = 8
HEAD_DIM = 64
A_DILATIONS = (1, 4, 16)
A_STEPS = 128
A_BLK = 128
A_QROWS = 128
A_SUPER = A_BLK * max(A_DILATIONS)
ATTN_SCALE = HEAD_DIM ** -0.5
POOL_SIZES = (2, 4, 8, 16)
POOL_HALO = 16
ML_HEADS = 4
ML_CONV = 4
ML_QKV_BLOCK = 4
ML_PROMPT_CHUNK = 256
ML_SAMPLE_PAD = 16
ML_HEADS_PER_STEP = 4
ML_NCOLS = 128
ML_SLAB = 128
FFN_CHUNK = 1024
HEADWISE_CHUNK = 256
RMS_EPS = 1e-6
LN_EPS = 1e-5
NEG = -1e30
VMEM_LIMIT = 56 * 1024 * 1024


def _cparams(*sem):
    return pltpu.CompilerParams(dimension_semantics=sem, vmem_limit_bytes=VMEM_LIMIT)


def _const_spec(shape):
    nd = len(shape)
    return pl.BlockSpec(shape, lambda *_: (0,) * nd, pipeline_mode=pl.Buffered(1))


def _rms(x, g):
    return x * lax.rsqrt(jnp.mean(x * x, axis=-1, keepdims=True) + RMS_EPS) * g


def _log_sigmoid(x):
    return jnp.minimum(x, 0.0) - jnp.log(1.0 + jnp.exp(-jnp.abs(x)))


def _ab_in_kernel(x_ref, g_ref, w_ref, q_ref, kf_ref, vf_ref, kb_ref, vb_ref, u_ref):
    aw = q_ref.shape[-1]
    xn = _rms(x_ref[...], g_ref[...]).astype(BF16)
    p = jnp.dot(xn, w_ref[...], preferred_element_type=F32)
    q_ref[...] = (p[:, :aw] * ATTN_SCALE).astype(BF16)
    k = p[:, aw:2 * aw]
    v = p[:, 2 * aw:3 * aw]
    kf_ref[...] = k
    vf_ref[...] = v
    kb_ref[...] = k.astype(BF16)
    vb_ref[...] = v.astype(BF16)
    u_ref[...] = p[:, 3 * aw:]


def _ab_in(x, g, w, tm, tiles_per_seq, keep):
    m, d = x.shape
    aw = A_HEADS * HEAD_DIM
    bw = w.shape[1] - 3 * aw
    row = lambda n: pl.BlockSpec((tm, n), lambda i: (i, 0))
    skip = tiles_per_seq - keep
    kept = pl.BlockSpec((tm, aw), lambda i: ((i // tiles_per_seq) * keep + jnp.maximum(i % tiles_per_seq - skip, 0), 0))
    mk = m // tiles_per_seq * keep
    return pl.pallas_call(
        _ab_in_kernel,
        grid=(m // tm,),
        in_specs=[row(d), _const_spec((1, d)), _const_spec(w.shape)],
        out_specs=[row(aw), kept, kept, row(aw), row(aw), row(bw)],
        out_shape=[jax.ShapeDtypeStruct((m, aw), BF16), jax.ShapeDtypeStruct((mk, aw), F32),
                   jax.ShapeDtypeStruct((mk, aw), F32), jax.ShapeDtypeStruct((m, aw), BF16),
                   jax.ShapeDtypeStruct((m, aw), BF16), jax.ShapeDtypeStruct((m, bw), F32)],
        compiler_params=_cparams("arbitrary"),
        name="ab_in",
    )(x, g, w)


def _attn_kernel(q_ref, kc_ref, kp_ref, vc_ref, vp_ref, a_ref, qs, ks, vs, os_, ms_, ds_, bias):
    sb = pl.program_id(1)
    qs[...] = q_ref[0].astype(F32)
    ks[0:A_SUPER, :] = kp_ref[0].astype(F32)
    ks[A_SUPER:, :] = kc_ref[0].astype(F32)
    vs[0:A_SUPER, :] = vp_ref[0].astype(F32)
    vs[A_SUPER:, :] = vc_ref[0].astype(F32)
    qi = lax.broadcasted_iota(jnp.int32, (A_BLK, 2 * A_BLK), 0)
    ki = lax.broadcasted_iota(jnp.int32, (A_BLK, 2 * A_BLK), 1)
    dist = qi - ki + A_BLK
    band = (dist >= 0) & (dist <= A_STEPS)
    bias[0] = jnp.where(band, 0.0, NEG)
    bias[1] = jnp.where(band & (ki >= A_BLK), 0.0, NEG)
    lane = lax.broadcasted_iota(jnp.int32, (A_QROWS, 2 * HEAD_DIM), 1)
    low = lane < HEAD_DIM
    nt = (((1,), (1,)), ((), ()))

    def rows(start, size, dil):
        return pl.ds(start, size) if dil == 1 else pl.ds(start, size, stride=dil)

    def scores(g, dil, r, n, part):
        rows_q = rows((n * A_BLK + part * A_QROWS) * dil + r, A_QROWS, dil)
        rows_k = rows(A_SUPER + (n - 1) * A_BLK * dil + r, 2 * A_BLK, dil)
        qp = qs[rows_q, :]
        kp = ks[rows_k, :].astype(BF16)
        vp = vs[rows_k, :].astype(BF16)
        which = (sb == 0).astype(jnp.int32) if n == 0 else 0
        mask_bias = bias[which, part * A_QROWS:(part + 1) * A_QROWS, :]
        q2 = jnp.concatenate([jnp.where(sel, qp, 0.0).astype(BF16) for sel in (low, ~low)], axis=0)
        s2 = lax.dot_general(q2, kp, nt, preferred_element_type=F32)
        s = [s2[j * A_QROWS:(j + 1) * A_QROWS] + mask_bias for j in range(2)]
        return g, rows_q, s, vp

    def weights(task):
        g, rows_q, s, vp = task
        maxs = [jnp.max(x, axis=-1, keepdims=True) for x in s]
        es = [jnp.exp(x - m) for x, m in zip(s, maxs)]
        dens = [jnp.sum(e, axis=-1, keepdims=True) for e in es]
        return g, rows_q, [e.astype(BF16) for e in es], maxs, dens, vp

    def values(task):
        g, rows_q, es, maxs, dens, vp = task
        acc2 = jnp.dot(jnp.concatenate(es, axis=0), vp, preferred_element_type=F32)
        accs = [acc2[j * A_QROWS:(j + 1) * A_QROWS] for j in range(2)]
        os_[g, rows_q, :] = jnp.where(low, accs[0], accs[1])
        ms_[g, rows_q, :] = jnp.where(low, maxs[0], maxs[1])
        ds_[g, rows_q, :] = jnp.where(low, dens[0], dens[1])

    tasks = [(g, dil, r, n, part) for g, dil in enumerate(A_DILATIONS)
             for r in range(dil) for n in range(A_SUPER // (A_BLK * dil)) for part in range(A_BLK // A_QROWS)]
    scored = {i: scores(*tasks[i]) for i in range(2)}
    weighted = {0: weights(scored.pop(0))}
    for i in range(len(tasks)):
        if i + 2 < len(tasks):
            scored[i + 2] = scores(*tasks[i + 2])
        if i + 1 < len(tasks):
            weighted[i + 1] = weights(scored.pop(i + 1))
        values(weighted.pop(i))

    ms = [ms_[g] for g in range(len(A_DILATIONS))]
    mm = functools.reduce(jnp.maximum, ms)
    es = [jnp.exp(m - mm) for m in ms]
    num = sum(e * os_[g] for g, e in enumerate(es))
    den = sum(e * ds_[g] for g, e in enumerate(es))
    a_ref[0] = (num / den).astype(a_ref.dtype)


def _attn(q, k, v):
    b, s, aw = q.shape
    pw = 2 * HEAD_DIM
    assert s % A_SUPER == 0
    cur = pl.BlockSpec((1, A_SUPER, pw), lambda bi, sb, p: (bi, sb, p))
    prev = pl.BlockSpec((1, A_SUPER, pw), lambda bi, sb, p: (bi, jnp.maximum(sb - 1, 0), p))
    nd = len(A_DILATIONS)
    return pl.pallas_call(
        _attn_kernel,
        grid=(b, s // A_SUPER, aw // pw),
        in_specs=[cur, cur, prev, cur, prev],
        out_specs=cur,
        out_shape=jax.ShapeDtypeStruct((b, s, aw), BF16),
        scratch_shapes=[pltpu.VMEM((A_SUPER, pw), F32), pltpu.VMEM((2 * A_SUPER, pw), F32),
                        pltpu.VMEM((2 * A_SUPER, pw), F32), pltpu.VMEM((nd, A_SUPER, pw), F32),
                        pltpu.VMEM((nd, A_SUPER, pw), F32), pltpu.VMEM((nd, A_SUPER, pw), F32),
                        pltpu.VMEM((2, A_BLK, 2 * A_BLK), F32)],
        compiler_params=_cparams("arbitrary", "arbitrary", "arbitrary"),
        name="attn",
    )(q, k, k, v, v)


def _sattn_kernel(q_ref, kn_ref, vn_ref, kt_ref, vt_ref, a_ref, *, t_len):
    nh, rows, _ = q_ref.shape[1:]
    buf = kt_ref.shape[-1]
    nd = len(A_DILATIONS)
    rnd = lambda x: x.astype(BF16).astype(F32)
    stack = lambda parts: jnp.concatenate(parts, axis=0)
    trow = lax.broadcasted_iota(jnp.int32, (rows, 1), 0)
    delta = (buf + lax.broadcasted_iota(jnp.int32, (rows, buf), 0)
             - lax.broadcasted_iota(jnp.int32, (rows, buf), 1))
    bias = stack([jnp.where((delta % dil == 0) & (delta <= A_STEPS * dil), 0.0, NEG) for dil in A_DILATIONS])
    for h in range(nh):
        qh = q_ref[0, h]
        knh = rnd(kn_ref[0, h])
        vnh = rnd(vn_ref[0, h])
        s_all = jnp.dot(qh.astype(BF16), kt_ref[0, h].astype(BF16), preferred_element_type=F32)
        s = stack([s_all] * nd) + bias
        s_n = []
        for tp in range(t_len):
            x = jnp.sum(qh * knh[tp:tp + 1, :], axis=-1, keepdims=True)
            s_n.append(stack([jnp.where((trow >= tp) & ((trow - tp) % dil == 0), x, NEG)
                              for dil in A_DILATIONS]))
        m = functools.reduce(jnp.maximum, s_n, jnp.max(s, axis=-1, keepdims=True))
        p = jnp.exp(s - m)
        p_n = [jnp.exp(x - m) for x in s_n]
        den = jnp.sum(p, axis=-1, keepdims=True) + sum(p_n)
        o = lax.dot_general(p.astype(BF16), vt_ref[0, h].astype(BF16), (((1,), (1,)), ((), ())),
                            preferred_element_type=F32)
        o = (o + sum(rnd(pn) * vnh[tp:tp + 1, :] for tp, pn in enumerate(p_n))) / den
        lse = m + jnp.log(den)
        group = lambda x, g: x[g * rows:(g + 1) * rows]
        mm = functools.reduce(jnp.maximum, [group(lse, g) for g in range(nd)])
        es = [jnp.exp(group(lse, g) - mm) for g in range(nd)]
        a_ref[0, h] = sum(e * group(o, g) for g, e in enumerate(es)) / sum(es)


def _sattn(q, kn, vn, kt, vt, t_len):
    buf = kt.shape[-1]
    assert buf >= A_STEPS * max(A_DILATIONS)
    spec = lambda t: pl.BlockSpec((1,) + t.shape[1:], lambda i: (i, 0, 0, 0))
    return pl.pallas_call(
        functools.partial(_sattn_kernel, t_len=t_len),
        grid=(q.shape[0],),
        in_specs=[spec(t) for t in (q, kn, vn, kt, vt)],
        out_specs=spec(q),
        out_shape=jax.ShapeDtypeStruct(q.shape, F32),
        compiler_params=_cparams("arbitrary"),
        name="sattn",
    )(q, kn, vn, kt, vt)


def _ab_out_kernel(a_ref, u_ref, halo_ref, wp_ref, sc_ref, woa_ref, wob_ref, res_ref, g_ref, w1_ref, w2_ref,
                   h_ref, ext_ref, *, g_seq, tiles_per_seq, start, fresh):
    tm, bw = u_ref.shape
    hrows = POOL_HALO * g_seq
    i = pl.program_id(0)
    a = a_ref[...]
    u = u_ref[...]
    halo = halo_ref[...]
    if fresh:
        halo = jnp.where(i % tiles_per_seq == 0, 0.0, halo)
    ext_ref[0:hrows, :] = halo
    ext_ref[hrows:hrows + tm, :] = u
    t_idx = lax.broadcasted_iota(jnp.int32, (tm, 1), 0) // g_seq
    pos = start + (i % tiles_per_seq) * (tm // g_seq) + t_idx
    gw = bw // len(POOL_SIZES)
    y = jnp.zeros((tm, woa_ref.shape[1]), F32)
    y += jnp.dot(a.astype(BF16), woa_ref[...], preferred_element_type=F32)
    for g, w in enumerate(POOL_SIZES):
        cols = slice(g * gw, (g + 1) * gw)
        acc = u[:, cols]
        for j in range(1, w):
            acc = acc + ext_ref[hrows - j * g_seq:hrows - j * g_seq + tm, cols]
        cnt = jnp.minimum(pos + 1, w).astype(F32)
        pooled = acc / cnt - u[:, cols]
        yg = jnp.dot(pooled.astype(BF16), wp_ref[g], preferred_element_type=F32) * sc_ref[:, cols]
        y += jnp.dot(yg.astype(BF16), wob_ref[cols, :], preferred_element_type=F32)
    h_ref[...] = _ffn_tail(res_ref[...] + y, g_ref, w1_ref, w2_ref, None)


def _ab_out(a, u, halo, halo_spec, wp, scale, wo_a, wo_b, res, g, w1, w2, *, tm, g_seq, tiles_per_seq, start,
            fresh):
    m, bw = u.shape
    d = res.shape[1]
    row = lambda n: pl.BlockSpec((tm, n), lambda i: (i, 0))
    kern = functools.partial(_ab_out_kernel, g_seq=g_seq, tiles_per_seq=tiles_per_seq, start=start, fresh=fresh)
    return pl.pallas_call(
        kern,
        grid=(m // tm,),
        in_specs=[row(a.shape[1]), row(bw), halo_spec, _const_spec(wp.shape), _const_spec(scale.shape),
                  _const_spec(wo_a.shape), _const_spec(wo_b.shape), row(d), _const_spec(g.shape),
                  _const_spec(w1.shape), _const_spec(w2.shape)],
        out_specs=row(d),
        out_shape=jax.ShapeDtypeStruct((m, d), F32),
        scratch_shapes=[pltpu.VMEM((POOL_HALO * g_seq + tm, bw), F32)],
        compiler_params=_cparams("arbitrary"),
        name="ab_out",
    )(a, u, halo, wp, scale, wo_a, wo_b, res, g, w1, w2)


def _ffn_tail(x, g_ref, w1_ref, w2_ref, gf_ref):
    xn = _rms(x, g_ref[...]).astype(BF16)
    acc = jnp.zeros(x.shape, F32)
    for c in range(w1_ref.shape[1] // FFN_CHUNK):
        cols = slice(c * FFN_CHUNK, (c + 1) * FFN_CHUNK)
        h = jnp.maximum(jnp.dot(xn, w1_ref[:, cols], preferred_element_type=F32), 0.0)
        acc += jnp.dot((h * h).astype(BF16), w2_ref[cols, :], preferred_element_type=F32)
    out = x + acc
    if gf_ref is not None:
        out = _rms(out, gf_ref[...])
    return out


def _ml_in_kernel(*refs, g_seq, tiles_per_seq, halo_steps, carry, chunk_rows):
    if carry:
        (x_ref, g_ref, win_ref, wc_ref, bc_ref, wq_ref, wk_ref, wv_ref, wg_ref, bg_ref, skip_ref,
         q_ref, k_ref, v_ref, gate_ref, grow_ref, sg_ref, gsk_ref, tail_ref, ext_ref) = refs
    else:
        (x_ref, g_ref, win_ref, halo_ref, wc_ref, bc_ref, wq_ref, wk_ref, wv_ref, wg_ref, bg_ref, skip_ref,
         q_ref, k_ref, v_ref, gate_ref, grow_ref, sg_ref, gsk_ref, tail_ref, ext_ref) = refs
    tm = x_ref.shape[0]
    inner = q_ref.shape[1]
    hd = inner // ML_HEADS
    hrows = halo_steps * g_seq
    i = pl.program_id(0)

    if carry:
        @pl.when(i % tiles_per_seq == 0)
        def _():
            ext_ref[0:hrows, :] = jnp.zeros((hrows, inner), F32)
    else:
        ext_ref[0:hrows, :] = halo_ref[...]

    xn = _rms(x_ref[...], g_ref[...]).astype(BF16)
    gates = jnp.zeros((tm, wg_ref.shape[1]), F32) + bg_ref[...]
    cw = HEADWISE_CHUNK
    def up_proj(c):
        return (jnp.dot(xn, win_ref[:, c * cw:(c + 1) * cw], preferred_element_type=F32),
                jnp.dot(xn, win_ref[:, inner + c * cw:inner + (c + 1) * cw], preferred_element_type=F32))

    def activate(c, xm, og):
        cols = slice(c * cw, (c + 1) * cw)
        ext_ref[hrows:hrows + tm, cols] = xm
        conv = xm * wc_ref[ML_CONV - 1:ML_CONV, cols] + bc_ref[:, cols]
        for j in range(ML_CONV - 1):
            off = hrows - (ML_CONV - 1 - j) * g_seq
            conv = conv + ext_ref[off:off + tm, cols] * wc_ref[j:j + 1, cols]
        tail = ext_ref[tm:tm + hrows, cols]
        tail_ref[0, :, cols] = tail
        if carry:
            ext_ref[0:hrows, cols] = tail
        ca = conv * jax.nn.sigmoid(conv)
        sig = jax.nn.sigmoid(og)
        sg_ref[:, cols] = sig.astype(BF16)
        gsk_ref[:, cols] = (skip_ref[:, cols] * ca * sig).astype(BF16)
        return ca.astype(BF16), xm.astype(BF16)

    nchunks = inner // cw
    ups = {c: up_proj(c) for c in range(min(2, nchunks))}
    acts = {0: activate(0, *ups.pop(0))}
    for c in range(nchunks):
        cols = slice(c * cw, (c + 1) * cw)
        if c + 2 < nchunks:
            ups[c + 2] = up_proj(c + 2)
        if c + 1 < nchunks:
            acts[c + 1] = activate(c + 1, *ups.pop(c + 1))
        ca_b, xm_b = acts.pop(c)
        qc = jnp.dot(ca_b, wq_ref[c], preferred_element_type=F32)
        kc = jnp.dot(ca_b, wk_ref[c], preferred_element_type=F32)
        vc = jnp.dot(xm_b, wv_ref[c], preferred_element_type=F32)
        qb, kb, vb = qc.astype(BF16), kc.astype(BF16), vc.astype(BF16)
        q_ref[:, cols] = qb
        k_ref[:, cols] = (kc * hd ** -0.5).astype(BF16)
        v_ref[:, cols] = vb
        gates += jnp.dot(qb, wg_ref[c * cw:(c + 1) * cw, :], preferred_element_type=F32)
        gates += jnp.dot(kb, wg_ref[inner + c * cw:inner + (c + 1) * cw, :], preferred_element_type=F32)
        gates += jnp.dot(vb, wg_ref[2 * inner + c * cw:2 * inner + (c + 1) * cw, :], preferred_element_type=F32)
    lane = lax.broadcasted_iota(jnp.int32, gates.shape, 1)
    row = lax.broadcasted_iota(jnp.int32, gates.shape, 0) % chunk_rows
    is_f = (lane >= ML_HEADS) & (lane < 2 * ML_HEADS)
    b = jnp.where(is_f, _log_sigmoid(gates), 0.0)
    shift = g_seq
    while shift < chunk_rows:
        b = b + jnp.where(row >= shift, pltpu.roll(b, shift, axis=0), 0.0)
        shift *= 2
    packed = jnp.where(is_f, b, gates)
    gate_ref[...] = packed
    for cc in range(tm // chunk_rows):
        grow_ref[cc] = jnp.transpose(packed[cc * chunk_rows:(cc + 1) * chunk_rows])[0:2 * ML_HEADS, :]


def _ml_in(x, g, w_in, halo, wc, bc, wq, wk, wv, wg, bg, skip, *, tm, g_seq, tiles_per_seq, halo_steps, chunk):
    m, d = x.shape
    chunk_rows = chunk * g_seq
    assert tm % chunk_rows == 0
    inner = wc.shape[1]
    carry = halo is None
    hrows = halo_steps * g_seq
    row = lambda n: pl.BlockSpec((tm, n), lambda i: (i, 0))
    consts = [wc, bc, wq, wk, wv, wg, bg, skip]
    args = [x, g, w_in] + ([] if carry else [halo]) + consts
    specs = ([row(d), _const_spec(g.shape), _const_spec(w_in.shape)]
             + ([] if carry else [_const_spec(halo.shape)]) + [_const_spec(t.shape) for t in consts])
    big = lambda dt: jax.ShapeDtypeStruct((m, inner), dt)
    return pl.pallas_call(
        functools.partial(_ml_in_kernel, g_seq=g_seq, tiles_per_seq=tiles_per_seq, halo_steps=halo_steps,
                          carry=carry, chunk_rows=chunk_rows),
        grid=(m // tm,),
        in_specs=specs,
        out_specs=[row(inner), row(inner), row(inner), row(wg.shape[1]),
                   pl.BlockSpec((tm // chunk_rows, 2 * ML_HEADS, chunk_rows), lambda i: (i, 0, 0)),
                   row(inner), row(inner),
                   pl.BlockSpec((1, hrows, inner), lambda i: (i, 0, 0))],
        out_shape=[big(BF16), big(BF16), big(BF16), jax.ShapeDtypeStruct((m, wg.shape[1]), F32),
                   jax.ShapeDtypeStruct((m // chunk_rows, 2 * ML_HEADS, chunk_rows), F32),
                   big(BF16), big(BF16), jax.ShapeDtypeStruct((m // tm, hrows, inner), F32)],
        scratch_shapes=[pltpu.VMEM((hrows + tm, inner), F32)],
        compiler_params=_cparams("arbitrary"),
        name="ml_in",
    )(*args)


def _ml_cell_kernel(q_ref, k_ref, v_ref, gcol_ref, grow_ref, c0_ref, n0_ref, m0_ref,
                    h_ref, co_ref, no_ref, mo_ref, cn_sc, m_sc):
    c = pl.program_id(2)
    nc = pl.num_programs(2)
    ln = q_ref.shape[1]
    hd = q_ref.shape[2] // ML_HEADS_PER_STEP
    s_i = lax.broadcasted_iota(jnp.int32, (ln, ln), 0)
    r_i = lax.broadcasted_iota(jnp.int32, (ln, ln), 1)
    causal = r_i <= s_i
    gcol = gcol_ref[0]
    glane = lax.broadcasted_iota(jnp.int32, gcol.shape, 1)

    @pl.when(c == 0)
    def _():
        for j in range(ML_HEADS_PER_STEP):
            cn_sc[j, :, 0:hd] = c0_ref[0, j]
            cn_sc[j, :, hd:] = jnp.transpose(jnp.broadcast_to(n0_ref[0, j], (ML_NCOLS, hd)))
            m_sc[j] = m0_ref[0, j]

    def read_stage(j):
        h = pl.program_id(1) * ML_HEADS_PER_STEP + j
        cols = slice(j * hd, (j + 1) * hd)
        q = q_ref[0, :, cols]
        ks = k_ref[0, :, cols]
        i_col = jnp.sum(jnp.where(glane == h, gcol, 0.0), axis=1, keepdims=True)
        b_col = jnp.sum(jnp.where(glane == h + ML_HEADS, gcol, 0.0), axis=1, keepdims=True)
        i_row = grow_ref[0, pl.ds(h, 1), :]
        b_row = grow_ref[0, pl.ds(h + ML_HEADS, 1), :]
        m_old = m_sc[j, :, 0:1]
        logw = jnp.where(causal, b_col - b_row + i_row, NEG)
        inter = b_col + m_old
        mt = jnp.maximum(inter, jnp.max(logw, axis=1, keepdims=True))
        scores = lax.dot_general(q, ks, (((1,), (1,)), ((), ())), preferred_element_type=F32)
        qcn = jnp.dot(q, cn_sc[j].astype(BF16), preferred_element_type=F32)
        return j, cols, ks, i_col, b_col, m_old, logw, inter, mt, scores, qcn

    def output_stage(st):
        j, cols, ks, i_col, b_col, m_old, logw, inter, mt, scores, qcn = st
        a = jnp.exp(logw - mt) * scores
        si = jnp.exp(inter - mt)
        num = si * qcn[:, 0:hd] + jnp.dot(a.astype(BF16), v_ref[0, :, cols], preferred_element_type=F32)
        den = si * qcn[:, hd:hd + 1] + jnp.sum(a, axis=1, keepdims=True)
        h_ref[0, :, cols] = (num / jnp.maximum(jnp.abs(den), jnp.exp(-mt))).astype(h_ref.dtype)

    def state_stage(st):
        j, cols, ks, i_col, b_col, m_old = st[:6]
        b_last = b_col[ln - 1:ln, :]
        wr = b_last - b_col + i_col
        m_new = jnp.maximum(b_last + m_old, jnp.max(wr, axis=0, keepdims=True))
        wk = (jnp.exp(wr - m_new) * ks.astype(F32)).astype(BF16)
        sc = jnp.exp(b_last + m_old - m_new)
        v = v_ref[0, :, cols]
        vaug = jnp.concatenate([v, jnp.ones((ln, ML_NCOLS), v.dtype)], axis=1)
        for r in range(hd // ML_SLAB):
            rows = slice(r * ML_SLAB, (r + 1) * ML_SLAB)
            upd = lax.dot_general(wk[:, rows], vaug, (((0,), (0,)), ((), ())), preferred_element_type=F32)
            cn_sc[j, rows, :] = sc * cn_sc[j, rows, :] + upd
        m_sc[j] = jnp.broadcast_to(m_new, m_sc.shape[1:])

    for j in range(ML_HEADS_PER_STEP):
        st = read_stage(j)
        output_stage(st)
        state_stage(st)

    @pl.when(c == nc - 1)
    def _():
        for j in range(ML_HEADS_PER_STEP):
            co_ref[0, j] = cn_sc[j, :, 0:hd]
            no_ref[0, j] = jnp.transpose(cn_sc[j, :, hd:])[0:1, :]
            mo_ref[0, j] = m_sc[j]


def _ml_cell(q, k, v, gcol, grow, c0, n0, m0, *, chunk):
    n, t, inner = q.shape
    hd = inner // ML_HEADS
    hps = ML_HEADS_PER_STEP
    nc = t // chunk
    tok = pl.BlockSpec((1, chunk, hps * hd), lambda b, h, c: (b, c, h))
    st = lambda r, w: pl.BlockSpec((1, hps, r, w), lambda b, h, c: (b, h, 0, 0))
    lanes = m0.shape[-1]
    return pl.pallas_call(
        _ml_cell_kernel,
        grid=(n, ML_HEADS // hps, nc),
        in_specs=[tok, tok, tok, pl.BlockSpec((1, chunk, gcol.shape[-1]), lambda b, h, c: (b, c, 0)),
                  pl.BlockSpec((1, 2 * ML_HEADS, chunk), lambda b, h, c: (b * nc + c, 0, 0)),
                  st(hd, hd), st(1, hd), st(1, lanes)],
        out_specs=[tok, st(hd, hd), st(1, hd), st(1, lanes)],
        out_shape=[jax.ShapeDtypeStruct((n, t, inner), BF16),
                   jax.ShapeDtypeStruct((n, ML_HEADS, hd, hd), F32),
                   jax.ShapeDtypeStruct((n, ML_HEADS, 1, hd), F32),
                   jax.ShapeDtypeStruct((n, ML_HEADS, 1, lanes), F32)],
        scratch_shapes=[pltpu.VMEM((hps, hd, hd + ML_NCOLS), F32), pltpu.VMEM((hps, 1, lanes), F32)],
        compiler_params=_cparams("arbitrary", "arbitrary", "arbitrary"),
        name="ml_cell",
    )(q, k, v, gcol, grow, c0, n0, m0)


def _ml_out_kernel(hc_ref, sg_ref, gsk_ref, gn_ref, w_ref, res_ref, g_ref, w1_ref, w2_ref, gf_ref, o_ref):
    hd = hc_ref.shape[1] // ML_HEADS
    h = res_ref[...]
    for j in range(ML_HEADS):
        cols = slice(j * hd, (j + 1) * hd)
        hc = hc_ref[:, cols].astype(F32)
        dev = hc - jnp.mean(hc, axis=1, keepdims=True)
        var = jnp.mean(dev * dev, axis=1, keepdims=True)
        hn = dev * lax.rsqrt(var + LN_EPS) * gn_ref[:, cols]
        y = hn * sg_ref[:, cols].astype(F32) + gsk_ref[:, cols].astype(F32)
        h = h + jnp.dot(y.astype(BF16), w_ref[cols, :], preferred_element_type=F32)
    o_ref[...] = _ffn_tail(h, g_ref, w1_ref, w2_ref, gf_ref)


def _ml_out(hc, sg, gsk, gn, w, res, g, w1, w2, gf, tm):
    m, kdim = hc.shape
    d = res.shape[1]
    row = lambda n: pl.BlockSpec((tm, n), lambda i: (i, 0))
    return pl.pallas_call(
        _ml_out_kernel,
        grid=(m // tm,),
        in_specs=[row(kdim), row(kdim), row(kdim), _const_spec(gn.shape), _const_spec(w.shape), row(d),
                  _const_spec(g.shape), _const_spec(w1.shape), _const_spec(w2.shape), _const_spec(gf.shape)],
        out_specs=row(d),
        out_shape=jax.ShapeDtypeStruct((m, d), F32),
        compiler_params=_cparams("arbitrary"),
        name="ml_out",
    )(hc, sg, gsk, gn, w, res, g, w1, w2, gf)


def _headwise_dense(w):
    blk = ML_QKV_BLOCK
    rows = w.astype(BF16).reshape(-1, HEADWISE_CHUNK, blk)
    col = jnp.arange(HEADWISE_CHUNK)
    spread = (col[None, :] % blk == jnp.arange(blk)[:, None]).astype(BF16)
    rep = jnp.einsum("crk,kn->crn", rows, spread, preferred_element_type=F32)
    return jnp.where(col[:, None] // blk == col[None, :] // blk, rep, 0.0).astype(BF16)


def _time_major(t):
    return jnp.swapaxes(t, 0, 1).reshape(-1, t.shape[-1])


def _seq_major(t, n):
    return jnp.swapaxes(t.reshape(-1, n, t.shape[-1]), 0, 1)


def kernel(x_prompt, x_sample, cache_a_k, cache_a_v, state_pool, state_ml_C, state_ml_n, state_ml_m, state_ml_conv, norm_mix, norm_ffn, norm_final, ab_w_in, ab_w_pool, ab_pool_scale, ab_w_out, ml_w_in, ml_w_conv, ml_b_conv, ml_w_q, ml_w_k, ml_w_v, ml_w_i, ml_b_i, ml_w_f, ml_b_f, ml_norm, ml_skip, ml_w_out, ffn_w1, ffn_w2):
    b, s, d = x_prompt.shape
    ns, ts, _ = x_sample.shape
    aw = A_HEADS * HEAD_DIM
    inner = ml_w_conv.shape[-1]
    hd = inner // ML_HEADS
    tm = 512
    row1 = lambda t: t.reshape(1, -1)

    hp = x_prompt.reshape(b * s, d)
    hs = _time_major(x_sample)
    tms = hs.shape[0]

    w_in = ab_w_in[0].astype(BF16)
    wp = ab_w_pool[0].astype(BF16)
    psc = row1(ab_pool_scale[0])
    wo = ab_w_out[0].astype(BF16)
    wo_a, wo_b = wo[:aw], wo[aw:]
    g0 = row1(norm_mix[0])
    ffn0 = (row1(norm_ffn[0]), ffn_w1[0].astype(BF16), ffn_w2[0].astype(BF16))

    tm0 = 2 * tm
    tps = s // tm0
    a_rows = min(A_STEPS * max(A_DILATIONS), s)
    assert a_rows % tm0 == 0
    q, kf, vf, kb, vb, u = _ab_in(hp, g0, w_in, tm0, tps, a_rows // tm0)
    to_seq = lambda t: t.reshape(b, s, aw)
    a_p = _attn(to_seq(q), to_seq(kb), to_seq(vb)).reshape(b * s, aw)
    halo_spec = pl.BlockSpec((POOL_HALO, u.shape[1]), lambda i: (jnp.maximum(i * (tm0 // POOL_HALO) - 1, 0), 0))
    hp = _ab_out(a_p, u, u, halo_spec, wp, psc, wo_a, wo_b, hp, *ffn0,
                 tm=tm0, g_seq=1, tiles_per_seq=tps, start=0, fresh=True)
    heads = lambda t: t.reshape(t.shape[0], -1, A_HEADS, HEAD_DIM)
    p_ak = heads(kf.reshape(b, a_rows, aw))[None]
    p_av = heads(vf.reshape(b, a_rows, aw))[None]
    pool_buf = state_pool.shape[2]
    p_pool = u.reshape(b, s, -1)[:, s - pool_buf:][None]

    qs, kfs, vfs, _, _, us = _ab_in(hs, g0, w_in, tms, 1, 1)
    s_ak = heads(_seq_major(kfs, ns))
    s_av = heads(_seq_major(vfs, ns))
    head_major = lambda t: jnp.pad(jnp.swapaxes(t, 1, 2), ((0, 0), (0, 0), (0, 8 - ts), (0, 0)))
    a_s = _sattn(head_major(heads(_seq_major(qs.astype(F32), ns))), head_major(s_ak), head_major(s_av),
                 jnp.transpose(cache_a_k[0], (0, 2, 3, 1)), jnp.transpose(cache_a_v[0], (0, 2, 3, 1)), ts)
    s_ak, s_av = s_ak[None], s_av[None]
    a_s = jnp.transpose(a_s[:, :, :ts], (2, 0, 1, 3)).reshape(ts * ns, aw)
    halo_s = jnp.pad(_time_major(state_pool[0]), ((ns * (POOL_HALO - pool_buf), 0), (0, 0)))
    hs = _ab_out(a_s, us, halo_s, _const_spec(halo_s.shape), wp, psc, wo_a, wo_b, hs, *ffn0,
                 tm=tms, g_seq=ns, tiles_per_seq=1, start=PAST_LEN, fresh=False)
    s_pool = jnp.concatenate([state_pool[0], _seq_major(us, ns)], axis=1)[:, ts:][None]

    g1 = row1(norm_mix[1])
    wi = ml_w_in[0].astype(BF16)
    wq, wk, wv = (_headwise_dense(t[0]) for t in (ml_w_q, ml_w_k, ml_w_v))
    glanes = 128
    wg = jnp.pad(jnp.concatenate([ml_w_i[0], ml_w_f[0]], axis=1), ((0, 0), (0, glanes - 2 * ML_HEADS))).astype(BF16)
    bg = jnp.pad(jnp.concatenate([ml_b_i[0], ml_b_f[0]]), (0, glanes - 2 * ML_HEADS)).reshape(1, glanes)
    consts = (ml_w_conv[0], row1(ml_b_conv[0]), wq, wk, wv, wg, bg, row1(ml_skip[0]))
    gn = row1(ml_norm[0])
    wout = ml_w_out[0].astype(BF16)
    conv_buf = ML_CONV - 1

    w1 = ffn_w1[1].astype(BF16)
    w2 = ffn_w2[1].astype(BF16)
    gf1 = row1(norm_ffn[1])
    gfin = row1(norm_final)

    tm1 = 2 * ML_PROMPT_CHUNK
    halo_p = 8
    q1, k1, v1, gc, gr, sg, gsk, tail = _ml_in(hp, g1, wi, None, *consts, tm=tm1, g_seq=1,
                                               tiles_per_seq=s // tm1, halo_steps=halo_p, chunk=ML_PROMPT_CHUNK)
    seq3 = lambda t: t.reshape(b, s, -1)
    zeros_state = (jnp.zeros((b, ML_HEADS, hd, hd), F32), jnp.zeros((b, ML_HEADS, 1, hd), F32),
                   jnp.zeros((b, ML_HEADS, 1, 128), F32))
    hc, p_c, p_n, p_m = _ml_cell(seq3(q1), seq3(k1), seq3(v1), seq3(gc), gr, *zeros_state, chunk=ML_PROMPT_CHUNK)
    y_prompt = _ml_out(hc.reshape(b * s, inner), sg, gsk, gn, wout, hp, gf1, w1, w2, gfin, tm).reshape(b, s, d)
    p_conv = tail.reshape(b, s // tm1, halo_p, inner)[:, -1, halo_p - conv_buf:][None]

    halo_c = jnp.pad(_time_major(state_ml_conv[0]), ((ns, 0), (0, 0)))
    q1, k1, v1, gc, gr, sg, gsk, tail = _ml_in(hs, g1, wi, halo_c, *consts, tm=tms, g_seq=ns,
                                               tiles_per_seq=1, halo_steps=conv_buf + 1, chunk=ts)
    extra = ML_SAMPLE_PAD - ts
    pad16 = lambda t: jnp.pad(_seq_major(t, ns), ((0, 0), (0, extra), (0, 0)))
    gc = jnp.pad(_seq_major(gc, ns), ((0, 0), (0, extra), (0, 0)), mode="edge")
    gc = jnp.where((jnp.arange(ML_SAMPLE_PAD)[:, None] >= ts) & (jnp.arange(gc.shape[-1]) < ML_HEADS), NEG, gc)
    gr = jnp.transpose(gc[:, :, :2 * ML_HEADS], (0, 2, 1))
    m0 = jnp.broadcast_to(state_ml_m[0][:, :, None, None], (ns, ML_HEADS, 1, 128))
    hcs, s_c, s_n, s_m = _ml_cell(pad16(q1), pad16(k1), pad16(v1), gc, gr,
                                  state_ml_C[0], state_ml_n[0][:, :, None, :], m0, chunk=ML_SAMPLE_PAD)
    y_sample = _seq_major(_ml_out(_time_major(hcs[:, :ts]), sg, gsk, gn, wout, hs, gf1, w1, w2, gfin, tms), ns)
    s_conv = _seq_major(tail[0], ns)[:, -conv_buf:][None]

    return (y_prompt, y_sample, p_ak, p_av, p_pool,
            p_c[None], p_n[:, :, 0][None], p_m[:, :, 0, 0][None], p_conv,
            s_ak, s_av, s_pool,
            s_c[None], s_n[:, :, 0][None], s_m[:, :, 0, 0][None], s_conv)
```

```python
import functools

import jax
import jax.numpy as jnp
from jax import lax
from jax.experimental import pallas as pl
from jax.experimental.pallas import tpu as pltpu

F32 = jnp.float32
BF16 = jnp.bfloat16

PAST_LEN = 16384
A_HEADS = 8
HEAD_DIM = 64
A_DILATIONS = (1, 4, 16)
A_STEPS = 128
A_BLK = 128
A_QROWS = 128
A_SUPER = A_BLK * max(A_DILATIONS)
ATTN_SCALE = HEAD_DIM ** -0.5
POOL_SIZES = (2, 4, 8, 16)
POOL_HALO = 16
ML_HEADS = 4
ML_CONV = 4
ML_QKV_BLOCK = 4
ML_PROMPT_CHUNK = 256
ML_SAMPLE_PAD = 16
ML_HEADS_PER_STEP = 4
ML_NCOLS = 128
ML_SLAB = 256
FFN_CHUNK = 1024
HEADWISE_CHUNK = 256
RMS_EPS = 1e-6
LN_EPS = 1e-5
NEG = -1e30
LANES = 128
SUBLANES = 8
VMEM_LIMIT = 56 * 1024 * 1024
TILE_LAYER0 = 1024
TILE_ML_IN = 2 * ML_PROMPT_CHUNK
TILE_ML_OUT = 512


def _cparams(*sem):
    return pltpu.CompilerParams(dimension_semantics=sem, vmem_limit_bytes=VMEM_LIMIT)


def _const_spec(shape):
    nd = len(shape)
    return pl.BlockSpec(shape, lambda *_: (0,) * nd, pipeline_mode=pl.Buffered(1))


def _rms(x, g):
    return x * lax.rsqrt(jnp.mean(x * x, axis=-1, keepdims=True) + RMS_EPS) * g


def _log_sigmoid(x):
    return jnp.minimum(x, 0.0) - jnp.log(1.0 + jnp.exp(-jnp.abs(x)))


def _ab_in_kernel(x_ref, g_ref, w_ref, q_ref, kf_ref, vf_ref, kb_ref, vb_ref, u_ref):
    aw = q_ref.shape[-1]
    xn = _rms(x_ref[...], g_ref[...]).astype(BF16)
    p = jnp.dot(xn, w_ref[...], preferred_element_type=F32)
    q_ref[...] = (p[:, :aw] * ATTN_SCALE).astype(BF16)
    k = p[:, aw:2 * aw]
    v = p[:, 2 * aw:3 * aw]
    kf_ref[...] = k
    vf_ref[...] = v
    kb_ref[...] = k.astype(BF16)
    vb_ref[...] = v.astype(BF16)
    u_ref[...] = p[:, 3 * aw:]


def _ab_in(x, g, w, tm, tiles_per_seq, keep):
    m, d = x.shape
    aw = A_HEADS * HEAD_DIM
    bw = w.shape[1] - 3 * aw
    row = lambda n: pl.BlockSpec((tm, n), lambda i: (i, 0))
    skip = tiles_per_seq - keep
    kept = pl.BlockSpec((tm, aw), lambda i: ((i // tiles_per_seq) * keep + jnp.maximum(i % tiles_per_seq - skip, 0), 0))
    mk = m // tiles_per_seq * keep
    return pl.pallas_call(
        _ab_in_kernel,
        grid=(m // tm,),
        in_specs=[row(d), _const_spec((1, d)), _const_spec(w.shape)],
        out_specs=[row(aw), kept, kept, row(aw), row(aw), row(bw)],
        out_shape=[jax.ShapeDtypeStruct((m, aw), BF16), jax.ShapeDtypeStruct((mk, aw), F32),
                   jax.ShapeDtypeStruct((mk, aw), F32), jax.ShapeDtypeStruct((m, aw), BF16),
                   jax.ShapeDtypeStruct((m, aw), BF16), jax.ShapeDtypeStruct((m, bw), F32)],
        compiler_params=_cparams("arbitrary"),
        name="ab_in",
    )(x, g, w)


def _attn_kernel(q_ref, kc_ref, kp_ref, vc_ref, vp_ref, a_ref, qs, ks, vs, os_, ms_, ds_, bias):
    sb = pl.program_id(1)
    qs[...] = q_ref[0].astype(F32)
    ks[0:A_SUPER, :] = kp_ref[0].astype(F32)
    ks[A_SUPER:, :] = kc_ref[0].astype(F32)
    vs[0:A_SUPER, :] = vp_ref[0].astype(F32)
    vs[A_SUPER:, :] = vc_ref[0].astype(F32)
    qi = lax.broadcasted_iota(jnp.int32, (A_BLK, 2 * A_BLK), 0)
    ki = lax.broadcasted_iota(jnp.int32, (A_BLK, 2 * A_BLK), 1)
    dist = qi - ki + A_BLK
    band = (dist >= 0) & (dist <= A_STEPS)
    bias[0] = jnp.where(band, 0.0, NEG)
    bias[1] = jnp.where(band & (ki >= A_BLK), 0.0, NEG)
    lane = lax.broadcasted_iota(jnp.int32, (A_QROWS, 2 * HEAD_DIM), 1)
    low = lane < HEAD_DIM
    nt = (((1,), (1,)), ((), ()))

    def rows(start, size, dil):
        return pl.ds(start, size) if dil == 1 else pl.ds(start, size, stride=dil)

    def scores(g, dil, r, n, part):
        rows_q = rows((n * A_BLK + part * A_QROWS) * dil + r, A_QROWS, dil)
        rows_k = rows(A_SUPER + (n - 1) * A_BLK * dil + r, 2 * A_BLK, dil)
        qp = qs[rows_q, :]
        kp = ks[rows_k, :].astype(BF16)
        vp = vs[rows_k, :].astype(BF16)
        which = (sb == 0).astype(jnp.int32) if n == 0 else 0
        mask_bias = bias[which, part * A_QROWS:(part + 1) * A_QROWS, :]
        q2 = jnp.concatenate([jnp.where(sel, qp, 0.0).astype(BF16) for sel in (low, ~low)], axis=0)
        s2 = lax.dot_general(q2, kp, nt, preferred_element_type=F32)
        s = [s2[j * A_QROWS:(j + 1) * A_QROWS] + mask_bias for j in range(2)]
        return g, rows_q, s, vp

    def weights(task):
        g, rows_q, s, vp = task
        maxs = [jnp.max(x, axis=-1, keepdims=True) for x in s]
        es = [jnp.exp(x - m) for x, m in zip(s, maxs)]
        dens = [jnp.sum(e, axis=-1, keepdims=True) for e in es]
        return g, rows_q, [e.astype(BF16) for e in es], maxs, dens, vp

    def values(task):
        g, rows_q, es, maxs, dens, vp = task
        acc2 = jnp.dot(jnp.concatenate(es, axis=0), vp, preferred_element_type=F32)
        accs = [acc2[j * A_QROWS:(j + 1) * A_QROWS] for j in range(2)]
        os_[g, rows_q, :] = jnp.where(low, accs[0], accs[1])
        ms_[g, rows_q, :] = jnp.where(low, maxs[0], maxs[1])
        ds_[g, rows_q, :] = jnp.where(low, dens[0], dens[1])

    tasks = [(g, dil, r, n, part) for g, dil in enumerate(A_DILATIONS)
             for r in range(dil) for n in range(A_SUPER // (A_BLK * dil)) for part in range(A_BLK // A_QROWS)]
    scored = {i: scores(*tasks[i]) for i in range(2)}
    weighted = {0: weights(scored.pop(0))}
    for i in range(len(tasks)):
        if i + 2 < len(tasks):
            scored[i + 2] = scores(*tasks[i + 2])
        if i + 1 < len(tasks):
            weighted[i + 1] = weights(scored.pop(i + 1))
        values(weighted.pop(i))

    ms = [ms_[g] for g in range(len(A_DILATIONS))]
    mm = functools.reduce(jnp.maximum, ms)
    es = [jnp.exp(m - mm) for m in ms]
    num = sum(e * os_[g] for g, e in enumerate(es))
    den = sum(e * ds_[g] for g, e in enumerate(es))
    a_ref[0] = (num / den).astype(a_ref.dtype)


def _attn(q, k, v):
    b, s, aw = q.shape
    pw = 2 * HEAD_DIM
    assert s % A_SUPER == 0
    cur = pl.BlockSpec((1, A_SUPER, pw), lambda bi, sb, p: (bi, sb, p))
    prev = pl.BlockSpec((1, A_SUPER, pw), lambda bi, sb, p: (bi, jnp.maximum(sb - 1, 0), p))
    nd = len(A_DILATIONS)
    return pl.pallas_call(
        _attn_kernel,
        grid=(b, s // A_SUPER, aw // pw),
        in_specs=[cur, cur, prev, cur, prev],
        out_specs=cur,
        out_shape=jax.ShapeDtypeStruct((b, s, aw), BF16),
        scratch_shapes=[pltpu.VMEM((A_SUPER, pw), F32), pltpu.VMEM((2 * A_SUPER, pw), F32),
                        pltpu.VMEM((2 * A_SUPER, pw), F32), pltpu.VMEM((nd, A_SUPER, pw), F32),
                        pltpu.VMEM((nd, A_SUPER, pw), F32), pltpu.VMEM((nd, A_SUPER, pw), F32),
                        pltpu.VMEM((2, A_BLK, 2 * A_BLK), F32)],
        compiler_params=_cparams("arbitrary", "arbitrary", "arbitrary"),
        name="attn",
    )(q, k, k, v, v)


def _sattn_kernel(q_ref, kn_ref, vn_ref, kt_ref, vt_ref, a_ref, *, t_len):
    nh, rows, _ = q_ref.shape[1:]
    buf = kt_ref.shape[-1]
    nd = len(A_DILATIONS)
    rnd = lambda x: x.astype(BF16).astype(F32)
    stack = lambda parts: jnp.concatenate(parts, axis=0)
    trow = lax.broadcasted_iota(jnp.int32, (rows, 1), 0)
    delta = (buf + lax.broadcasted_iota(jnp.int32, (rows, buf), 0)
             - lax.broadcasted_iota(jnp.int32, (rows, buf), 1))
    bias = stack([jnp.where((delta % dil == 0) & (delta <= A_STEPS * dil), 0.0, NEG) for dil in A_DILATIONS])
    for h in range(nh):
        qh = q_ref[0, h]
        knh = rnd(kn_ref[0, h])
        vnh = rnd(vn_ref[0, h])
        s_all = jnp.dot(qh.astype(BF16), kt_ref[0, h].astype(BF16), preferred_element_type=F32)
        s = stack([s_all] * nd) + bias
        s_n = []
        for tp in range(t_len):
            x = jnp.sum(qh * knh[tp:tp + 1, :], axis=-1, keepdims=True)
            s_n.append(stack([jnp.where((trow >= tp) & ((trow - tp) % dil == 0), x, NEG)
                              for dil in A_DILATIONS]))
        m = functools.reduce(jnp.maximum, s_n, jnp.max(s, axis=-1, keepdims=True))
        p = jnp.exp(s - m)
        p_n = [jnp.exp(x - m) for x in s_n]
        den = jnp.sum(p, axis=-1, keepdims=True) + sum(p_n)
        o = lax.dot_general(p.astype(BF16), vt_ref[0, h].astype(BF16), (((1,), (1,)), ((), ())),
                            preferred_element_type=F32)
        o = (o + sum(rnd(pn) * vnh[tp:tp + 1, :] for tp, pn in enumerate(p_n))) / den
        lse = m + jnp.log(den)
        group = lambda x, g: x[g * rows:(g + 1) * rows]
        mm = functools.reduce(jnp.maximum, [group(lse, g) for g in range(nd)])
        es = [jnp.exp(group(lse, g) - mm) for g in range(nd)]
        a_ref[0, h] = sum(e * group(o, g) for g, e in enumerate(es)) / sum(es)


def _sattn(q, kn, vn, kt, vt, t_len):
    buf = kt.shape[-1]
    assert buf >= A_STEPS * max(A_DILATIONS)
    spec = lambda t: pl.BlockSpec((1,) + t.shape[1:], lambda i: (i, 0, 0, 0))
    return pl.pallas_call(
        functools.partial(_sattn_kernel, t_len=t_len),
        grid=(q.shape[0],),
        in_specs=[spec(t) for t in (q, kn, vn, kt, vt)],
        out_specs=spec(q),
        out_shape=jax.ShapeDtypeStruct(q.shape, F32),
        compiler_params=_cparams("arbitrary"),
        name="sattn",
    )(q, kn, vn, kt, vt)


def _ab_out_kernel(a_ref, u_ref, halo_ref, wp_ref, sc_ref, woa_ref, wob_ref, res_ref, g_ref, w1_ref, w2_ref,
                   h_ref, ext_ref, *, g_seq, tiles_per_seq, start, fresh):
    tm, bw = u_ref.shape
    hrows = POOL_HALO * g_seq
    i = pl.program_id(0)
    a = a_ref[...]
    u = u_ref[...]
    halo = halo_ref[...]
    if fresh:
        halo = jnp.where(i % tiles_per_seq == 0, 0.0, halo)
    ext_ref[0:hrows, :] = halo
    ext_ref[hrows:hrows + tm, :] = u
    t_idx = lax.broadcasted_iota(jnp.int32, (tm, 1), 0) // g_seq
    pos = start + (i % tiles_per_seq) * (tm // g_seq) + t_idx
    gw = bw // len(POOL_SIZES)
    y = jnp.zeros((tm, woa_ref.shape[1]), F32)
    y += jnp.dot(a.astype(BF16), woa_ref[...], preferred_element_type=F32)
    for g, w in enumerate(POOL_SIZES):
        cols = slice(g * gw, (g + 1) * gw)
        acc = u[:, cols]
        for j in range(1, w):
            acc = acc + ext_ref[hrows - j * g_seq:hrows - j * g_seq + tm, cols]
        cnt = jnp.minimum(pos + 1, w).astype(F32)
        pooled = acc / cnt - u[:, cols]
        yg = jnp.dot(pooled.astype(BF16), wp_ref[g], preferred_element_type=F32) * sc_ref[:, cols]
        y += jnp.dot(yg.astype(BF16), wob_ref[cols, :], preferred_element_type=F32)
    h_ref[...] = _ffn_tail(res_ref[...] + y, g_ref, w1_ref, w2_ref, None)


def _ab_out(a, u, halo, halo_spec, wp, scale, wo_a, wo_b, res, g, w1, w2, *, tm, g_seq, tiles_per_seq, start,
            fresh):
    m, bw = u.shape
    d = res.shape[1]
    row = lambda n: pl.BlockSpec((tm, n), lambda i: (i, 0))
    kern = functools.partial(_ab_out_kernel, g_seq=g_seq, tiles_per_seq=tiles_per_seq, start=start, fresh=fresh)
    return pl.pallas_call(
        kern,
        grid=(m // tm,),
        in_specs=[row(a.shape[1]), row(bw), halo_spec, _const_spec(wp.shape), _const_spec(scale.shape),
                  _const_spec(wo_a.shape), _const_spec(wo_b.shape), row(d), _const_spec(g.shape),
                  _const_spec(w1.shape), _const_spec(w2.shape)],
        out_specs=row(d),
        out_shape=jax.ShapeDtypeStruct((m, d), F32),
        scratch_shapes=[pltpu.VMEM((POOL_HALO * g_seq + tm, bw), F32)],
        compiler_params=_cparams("arbitrary"),
        name="ab_out",
    )(a, u, halo, wp, scale, wo_a, wo_b, res, g, w1, w2)


def _ffn_tail(x, g_ref, w1_ref, w2_ref, gf_ref):
    xn = _rms(x, g_ref[...]).astype(BF16)
    acc = jnp.zeros(x.shape, F32)
    for c in range(w1_ref.shape[1] // FFN_CHUNK):
        cols = slice(c * FFN_CHUNK, (c + 1) * FFN_CHUNK)
        h = jnp.maximum(jnp.dot(xn, w1_ref[:, cols], preferred_element_type=F32), 0.0)
        acc += jnp.dot((h * h).astype(BF16), w2_ref[cols, :], preferred_element_type=F32)
    out = x + acc
    if gf_ref is not None:
        out = _rms(out, gf_ref[...])
    return out


def _ml_in_kernel(*refs, g_seq, tiles_per_seq, halo_steps, carry, chunk_rows):
    if carry:
        (x_ref, g_ref, win_ref, wc_ref, bc_ref, wq_ref, wk_ref, wv_ref, wg_ref, bg_ref, skip_ref,
         q_ref, k_ref, v_ref, gate_ref, grow_ref, sg_ref, gsk_ref, tail_ref, ext_ref) = refs
    else:
        (x_ref, g_ref, win_ref, halo_ref, wc_ref, bc_ref, wq_ref, wk_ref, wv_ref, wg_ref, bg_ref, skip_ref,
         q_ref, k_ref, v_ref, gate_ref, grow_ref, sg_ref, gsk_ref, tail_ref, ext_ref) = refs
    tm = x_ref.shape[0]
    inner = q_ref.shape[1]
    hd = inner // ML_HEADS
    hrows = halo_steps * g_seq
    i = pl.program_id(0)

    if carry:
        @pl.when(i % tiles_per_seq == 0)
        def _():
            ext_ref[0:hrows, :] = jnp.zeros((hrows, inner), F32)
    else:
        ext_ref[0:hrows, :] = halo_ref[...]

    xn = _rms(x_ref[...], g_ref[...]).astype(BF16)
    gates = jnp.zeros((tm, wg_ref.shape[1]), F32) + bg_ref[...]
    cw = HEADWISE_CHUNK
    def up_proj(c):
        return (jnp.dot(xn, win_ref[:, c * cw:(c + 1) * cw], preferred_element_type=F32),
                jnp.dot(xn, win_ref[:, inner + c * cw:inner + (c + 1) * cw], preferred_element_type=F32))

    def activate(c, xm, og):
        cols = slice(c * cw, (c + 1) * cw)
        ext_ref[hrows:hrows + tm, cols] = xm
        conv = xm * wc_ref[ML_CONV - 1:ML_CONV, cols] + bc_ref[:, cols]
        for j in range(ML_CONV - 1):
            off = hrows - (ML_CONV - 1 - j) * g_seq
            conv = conv + ext_ref[off:off + tm, cols] * wc_ref[j:j + 1, cols]
        tail = ext_ref[tm:tm + hrows, cols]
        tail_ref[0, :, cols] = tail
        if carry:
            ext_ref[0:hrows, cols] = tail
        ca = conv * jax.nn.sigmoid(conv)
        sig = jax.nn.sigmoid(og)
        sg_ref[:, cols] = sig.astype(BF16)
        gsk_ref[:, cols] = (skip_ref[:, cols] * ca * sig).astype(BF16)
        return ca.astype(BF16), xm.astype(BF16)

    nchunks = inner // cw
    ups = {c: up_proj(c) for c in range(min(2, nchunks))}
    acts = {0: activate(0, *ups.pop(0))}
    for c in range(nchunks):
        cols = slice(c * cw, (c + 1) * cw)
        if c + 2 < nchunks:
            ups[c + 2] = up_proj(c + 2)
        if c + 1 < nchunks:
            acts[c + 1] = activate(c + 1, *ups.pop(c + 1))
        ca_b, xm_b = acts.pop(c)
        qc = jnp.dot(ca_b, wq_ref[c], preferred_element_type=F32)
        kc = jnp.dot(ca_b, wk_ref[c], preferred_element_type=F32)
        vc = jnp.dot(xm_b, wv_ref[c], preferred_element_type=F32)
        qb, kb, vb = qc.astype(BF16), kc.astype(BF16), vc.astype(BF16)
        q_ref[:, cols] = qb
        k_ref[:, cols] = (kc * hd ** -0.5).astype(BF16)
        v_ref[:, cols] = vb
        gates += jnp.dot(qb, wg_ref[c * cw:(c + 1) * cw, :], preferred_element_type=F32)
        gates += jnp.dot(kb, wg_ref[inner + c * cw:inner + (c + 1) * cw, :], preferred_element_type=F32)
        gates += jnp.dot(vb, wg_ref[2 * inner + c * cw:2 * inner + (c + 1) * cw, :], preferred_element_type=F32)
    lane = lax.broadcasted_iota(jnp.int32, gates.shape, 1)
    row = lax.broadcasted_iota(jnp.int32, gates.shape, 0) % chunk_rows
    is_f = (lane >= ML_HEADS) & (lane < 2 * ML_HEADS)
    b = jnp.where(is_f, _log_sigmoid(gates), 0.0)
    shift = g_seq
    while shift < chunk_rows:
        b = b + jnp.where(row >= shift, pltpu.roll(b, shift, axis=0), 0.0)
        shift *= 2
    packed = jnp.where(is_f, b, gates)
    gate_ref[...] = packed
    for cc in range(tm // chunk_rows):
        grow_ref[cc] = jnp.transpose(packed[cc * chunk_rows:(cc + 1) * chunk_rows])[0:2 * ML_HEADS, :]


def _ml_in(x, g, w_in, halo, wc, bc, wq, wk, wv, wg, bg, skip, *, tm, g_seq, tiles_per_seq, halo_steps, chunk):
    m, d = x.shape
    chunk_rows = chunk * g_seq
    assert tm % chunk_rows == 0
    inner = wc.shape[1]
    carry = halo is None
    hrows = halo_steps * g_seq
    row = lambda n: pl.BlockSpec((tm, n), lambda i: (i, 0))
    consts = [wc, bc, wq, wk, wv, wg, bg, skip]
    args = [x, g, w_in] + ([] if carry else [halo]) + consts
    specs = ([row(d), _const_spec(g.shape), _const_spec(w_in.shape)]
             + ([] if carry else [_const_spec(halo.shape)]) + [_const_spec(t.shape) for t in consts])
    big = lambda dt: jax.ShapeDtypeStruct((m, inner), dt)
    return pl.pallas_call(
        functools.partial(_ml_in_kernel, g_seq=g_seq, tiles_per_seq=tiles_per_seq, halo_steps=halo_steps,
                          carry=carry, chunk_rows=chunk_rows),
        grid=(m // tm,),
        in_specs=specs,
        out_specs=[row(inner), row(inner), row(inner), row(wg.shape[1]),
                   pl.BlockSpec((tm // chunk_rows, 2 * ML_HEADS, chunk_rows), lambda i: (i, 0, 0)),
                   row(inner), row(inner),
                   pl.BlockSpec((1, hrows, inner), lambda i: (i, 0, 0))],
        out_shape=[big(BF16), big(BF16), big(BF16), jax.ShapeDtypeStruct((m, wg.shape[1]), F32),
                   jax.ShapeDtypeStruct((m // chunk_rows, 2 * ML_HEADS, chunk_rows), F32),
                   big(BF16), big(BF16), jax.ShapeDtypeStruct((m // tm, hrows, inner), F32)],
        scratch_shapes=[pltpu.VMEM((hrows + tm, inner), F32)],
        compiler_params=_cparams("arbitrary"),
        name="ml_in",
    )(*args)


def _ml_cell_kernel(q_ref, k_ref, v_ref, gcol_ref, grow_ref, c0_ref, n0_ref, m0_ref,
                    h_ref, co_ref, no_ref, mo_ref, cn_sc, m_sc):
    c = pl.program_id(2)
    nc = pl.num_programs(2)
    ln = q_ref.shape[1]
    hd = q_ref.shape[2] // ML_HEADS_PER_STEP
    s_i = lax.broadcasted_iota(jnp.int32, (ln, ln), 0)
    r_i = lax.broadcasted_iota(jnp.int32, (ln, ln), 1)
    causal = r_i <= s_i
    gcol = gcol_ref[0]
    glane = lax.broadcasted_iota(jnp.int32, gcol.shape, 1)

    @pl.when(c == 0)
    def _():
        for j in range(ML_HEADS_PER_STEP):
            cn_sc[j, :, 0:hd] = c0_ref[0, j]
            cn_sc[j, :, hd:] = jnp.transpose(jnp.broadcast_to(n0_ref[0, j], (ML_NCOLS, hd)))
            m_sc[j] = m0_ref[0, j]

    def read_stage(j):
        h = pl.program_id(1) * ML_HEADS_PER_STEP + j
        cols = slice(j * hd, (j + 1) * hd)
        q = q_ref[0, :, cols]
        ks = k_ref[0, :, cols]
        i_col = jnp.sum(jnp.where(glane == h, gcol, 0.0), axis=1, keepdims=True)
        b_col = jnp.sum(jnp.where(glane == h + ML_HEADS, gcol, 0.0), axis=1, keepdims=True)
        i_row = grow_ref[0, pl.ds(h, 1), :]
        b_row = grow_ref[0, pl.ds(h + ML_HEADS, 1), :]
        m_old = m_sc[j, :, 0:1]
        logw = jnp.where(causal, b_col - b_row + i_row, NEG)
        inter = b_col + m_old
        mt = jnp.maximum(inter, jnp.max(logw, axis=1, keepdims=True))
        scores = lax.dot_general(q, ks, (((1,), (1,)), ((), ())), preferred_element_type=F32)
        qcn = jnp.dot(q, cn_sc[j].astype(BF16), preferred_element_type=F32)
        return j, cols, ks, i_col, b_col, m_old, logw, inter, mt, scores, qcn

    def output_stage(st):
        j, cols, ks, i_col, b_col, m_old, logw, inter, mt, scores, qcn = st
        a = jnp.exp(logw - mt) * scores
        si = jnp.exp(inter - mt)
        num = si * qcn[:, 0:hd] + jnp.dot(a.astype(BF16), v_ref[0, :, cols], preferred_element_type=F32)
        den = si * qcn[:, hd:hd + 1] + jnp.sum(a, axis=1, keepdims=True)
        h_ref[0, :, cols] = (num / jnp.maximum(jnp.abs(den), jnp.exp(-mt))).astype(h_ref.dtype)

    def state_stage(st):
        j, cols, ks, i_col, b_col, m_old = st[:6]
        b_last = b_col[ln - 1:ln, :]
        wr = b_last - b_col + i_col
        m_new = jnp.maximum(b_last + m_old, jnp.max(wr, axis=0, keepdims=True))
        wk = (jnp.exp(wr - m_new) * ks.astype(F32)).astype(BF16)
        sc = jnp.exp(b_last + m_old - m_new)
        v = v_ref[0, :, cols]
        vaug = jnp.concatenate([v, jnp.ones((ln, ML_NCOLS), v.dtype)], axis=1)
        for r in range(hd // ML_SLAB):
            rows = slice(r * ML_SLAB, (r + 1) * ML_SLAB)
            upd = lax.dot_general(wk[:, rows], vaug, (((0,), (0,)), ((), ())), preferred_element_type=F32)
            cn_sc[j, rows, :] = sc * cn_sc[j, rows, :] + upd
        m_sc[j] = jnp.broadcast_to(m_new, m_sc.shape[1:])

    for j in range(ML_HEADS_PER_STEP):
        st = read_stage(j)
        output_stage(st)
        state_stage(st)

    @pl.when(c == nc - 1)
    def _():
        for j in range(ML_HEADS_PER_STEP):
            co_ref[0, j] = cn_sc[j, :, 0:hd]
            no_ref[0, j] = jnp.transpose(cn_sc[j, :, hd:])[0:1, :]
            mo_ref[0, j] = m_sc[j]


def _ml_cell(q, k, v, gcol, grow, c0, n0, m0, *, chunk):
    n, t, inner = q.shape
    hd = inner // ML_HEADS
    hps = ML_HEADS_PER_STEP
    nc = t // chunk
    tok = pl.BlockSpec((1, chunk, hps * hd), lambda b, h, c: (b, c, h))
    st = lambda r, w: pl.BlockSpec((1, hps, r, w), lambda b, h, c: (b, h, 0, 0))
    lanes = m0.shape[-1]
    return pl.pallas_call(
        _ml_cell_kernel,
        grid=(n, ML_HEADS // hps, nc),
        in_specs=[tok, tok, tok, pl.BlockSpec((1, chunk, gcol.shape[-1]), lambda b, h, c: (b, c, 0)),
                  pl.BlockSpec((1, 2 * ML_HEADS, chunk), lambda b, h, c: (b * nc + c, 0, 0)),
                  st(hd, hd), st(1, hd), st(1, lanes)],
        out_specs=[tok, st(hd, hd), st(1, hd), st(1, lanes)],
        out_shape=[jax.ShapeDtypeStruct((n, t, inner), BF16),
                   jax.ShapeDtypeStruct((n, ML_HEADS, hd, hd), F32),
                   jax.ShapeDtypeStruct((n, ML_HEADS, 1, hd), F32),
                   jax.ShapeDtypeStruct((n, ML_HEADS, 1, lanes), F32)],
        scratch_shapes=[pltpu.VMEM((hps, hd, hd + ML_NCOLS), F32), pltpu.VMEM((hps, 1, lanes), F32)],
        compiler_params=_cparams("arbitrary", "arbitrary", "arbitrary"),
        name="ml_cell",
    )(q, k, v, gcol, grow, c0, n0, m0)


def _ml_out_kernel(hc_ref, sg_ref, gsk_ref, gn_ref, w_ref, res_ref, g_ref, w1_ref, w2_ref, gf_ref, o_ref):
    hd = hc_ref.shape[1] // ML_HEADS
    h = res_ref[...]
    for j in range(ML_HEADS):
        cols = slice(j * hd, (j + 1) * hd)
        hc = hc_ref[:, cols].astype(F32)
        dev = hc - jnp.mean(hc, axis=1, keepdims=True)
        var = jnp.mean(dev * dev, axis=1, keepdims=True)
        hn = dev * lax.rsqrt(var + LN_EPS) * gn_ref[:, cols]
        y = hn * sg_ref[:, cols].astype(F32) + gsk_ref[:, cols].astype(F32)
        h = h + jnp.dot(y.astype(BF16), w_ref[cols, :], preferred_element_type=F32)
    o_ref[...] = _ffn_tail(h, g_ref, w1_ref, w2_ref, gf_ref)


def _ml_out(hc, sg, gsk, gn, w, res, g, w1, w2, gf, tm):
    m, kdim = hc.shape
    d = res.shape[1]
    row = lambda n: pl.BlockSpec((tm, n), lambda i: (i, 0))
    return pl.pallas_call(
        _ml_out_kernel,
        grid=(m // tm,),
        in_specs=[row(kdim), row(kdim), row(kdim), _const_spec(gn.shape), _const_spec(w.shape), row(d),
                  _const_spec(g.shape), _const_spec(w1.shape), _const_spec(w2.shape), _const_spec(gf.shape)],
        out_specs=row(d),
        out_shape=jax.ShapeDtypeStruct((m, d), F32),
        compiler_params=_cparams("arbitrary"),
        name="ml_out",
    )(hc, sg, gsk, gn, w, res, g, w1, w2, gf)


def _headwise_dense(w):
    blk = ML_QKV_BLOCK
    rows = w.astype(BF16).reshape(-1, HEADWISE_CHUNK, blk)
    col = jnp.arange(HEADWISE_CHUNK)
    spread = (col[None, :] % blk == jnp.arange(blk)[:, None]).astype(BF16)
    rep = jnp.einsum("crk,kn->crn", rows, spread, preferred_element_type=F32)
    return jnp.where(col[:, None] // blk == col[None, :] // blk, rep, 0.0).astype(BF16)


def _time_major(t):
    return jnp.swapaxes(t, 0, 1).reshape(-1, t.shape[-1])


def _seq_major(t, n):
    return jnp.swapaxes(t.reshape(-1, n, t.shape[-1]), 0, 1)


def kernel(x_prompt, x_sample, cache_a_k, cache_a_v, state_pool, state_ml_C, state_ml_n, state_ml_m, state_ml_conv, norm_mix, norm_ffn, norm_final, ab_w_in, ab_w_pool, ab_pool_scale, ab_w_out, ml_w_in, ml_w_conv, ml_b_conv, ml_w_q, ml_w_k, ml_w_v, ml_w_i, ml_b_i, ml_w_f, ml_b_f, ml_norm, ml_skip, ml_w_out, ffn_w1, ffn_w2):
    b, s, d = x_prompt.shape
    ns, ts, _ = x_sample.shape
    aw = A_HEADS * HEAD_DIM
    inner = ml_w_conv.shape[-1]
    hd = inner // ML_HEADS
    row1 = lambda t: t.reshape(1, -1)

    hp = x_prompt.reshape(b * s, d)
    hs = _time_major(x_sample)
    tms = hs.shape[0]

    w_in = ab_w_in[0].astype(BF16)
    wp = ab_w_pool[0].astype(BF16)
    psc = row1(ab_pool_scale[0])
    wo = ab_w_out[0].astype(BF16)
    wo_a, wo_b = wo[:aw], wo[aw:]
    g0 = row1(norm_mix[0])
    ffn0 = (row1(norm_ffn[0]), ffn_w1[0].astype(BF16), ffn_w2[0].astype(BF16))

    tm0 = min(TILE_LAYER0, s)
    tps = s // tm0
    a_rows = min(A_STEPS * max(A_DILATIONS), s)
    assert a_rows % tm0 == 0
    q, kf, vf, kb, vb, u = _ab_in(hp, g0, w_in, tm0, tps, a_rows // tm0)
    to_seq = lambda t: t.reshape(b, s, aw)
    a_p = _attn(to_seq(q), to_seq(kb), to_seq(vb)).reshape(b * s, aw)
    halo_spec = pl.BlockSpec((POOL_HALO, u.shape[1]), lambda i: (jnp.maximum(i * (tm0 // POOL_HALO) - 1, 0), 0))
    hp = _ab_out(a_p, u, u, halo_spec, wp, psc, wo_a, wo_b, hp, *ffn0,
                 tm=tm0, g_seq=1, tiles_per_seq=tps, start=0, fresh=True)
    heads = lambda t: t.reshape(t.shape[0], -1, A_HEADS, HEAD_DIM)
    p_ak = heads(kf.reshape(b, a_rows, aw))[None]
    p_av = heads(vf.reshape(b, a_rows, aw))[None]
    pool_buf = state_pool.shape[2]
    p_pool = u.reshape(b, s, -1)[:, s - pool_buf:][None]

    qs, kfs, vfs, _, _, us = _ab_in(hs, g0, w_in, tms, 1, 1)
    s_ak = heads(_seq_major(kfs, ns))
    s_av = heads(_seq_major(vfs, ns))
    head_major = lambda t: jnp.pad(jnp.swapaxes(t, 1, 2), ((0, 0), (0, 0), (0, SUBLANES - ts), (0, 0)))
    a_s = _sattn(head_major(heads(_seq_major(qs.astype(F32), ns))), head_major(s_ak), head_major(s_av),
                 jnp.transpose(cache_a_k[0], (0, 2, 3, 1)), jnp.transpose(cache_a_v[0], (0, 2, 3, 1)), ts)
    s_ak, s_av = s_ak[None], s_av[None]
    a_s = jnp.transpose(a_s[:, :, :ts], (2, 0, 1, 3)).reshape(ts * ns, aw)
    halo_s = jnp.pad(_time_major(state_pool[0]), ((ns * (POOL_HALO - pool_buf), 0), (0, 0)))
    hs = _ab_out(a_s, us, halo_s, _const_spec(halo_s.shape), wp, psc, wo_a, wo_b, hs, *ffn0,
                 tm=tms, g_seq=ns, tiles_per_seq=1, start=PAST_LEN, fresh=False)
    s_pool = jnp.concatenate([state_pool[0], _seq_major(us, ns)], axis=1)[:, ts:][None]

    g1 = row1(norm_mix[1])
    wi = ml_w_in[0].astype(BF16)
    wq, wk, wv = (_headwise_dense(t[0]) for t in (ml_w_q, ml_w_k, ml_w_v))
    glanes = LANES
    wg = jnp.pad(jnp.concatenate([ml_w_i[0], ml_w_f[0]], axis=1), ((0, 0), (0, glanes - 2 * ML_HEADS))).astype(BF16)
    bg = jnp.pad(jnp.concatenate([ml_b_i[0], ml_b_f[0]]), (0, glanes - 2 * ML_HEADS)).reshape(1, glanes)
    consts = (ml_w_conv[0], row1(ml_b_conv[0]), wq, wk, wv, wg, bg, row1(ml_skip[0]))
    gn = row1(ml_norm[0])
    wout = ml_w_out[0].astype(BF16)
    conv_buf = ML_CONV - 1

    w1 = ffn_w1[1].astype(BF16)
    w2 = ffn_w2[1].astype(BF16)
    gf1 = row1(norm_ffn[1])
    gfin = row1(norm_final)

    tm1 = TILE_ML_IN
    halo_p = SUBLANES
    q1, k1, v1, gc, gr, sg, gsk, tail = _ml_in(hp, g1, wi, None, *consts, tm=tm1, g_seq=1,
                                               tiles_per_seq=s // tm1, halo_steps=halo_p, chunk=ML_PROMPT_CHUNK)
    seq3 = lambda t: t.reshape(b, s, -1)
    zeros_state = (jnp.zeros((b, ML_HEADS, hd, hd), F32), jnp.zeros((b, ML_HEADS, 1, hd), F32),
                   jnp.zeros((b, ML_HEADS, 1, LANES), F32))
    hc, p_c, p_n, p_m = _ml_cell(seq3(q1), seq3(k1), seq3(v1), seq3(gc), gr, *zeros_state, chunk=ML_PROMPT_CHUNK)
    y_prompt = _ml_out(hc.reshape(b * s, inner), sg, gsk, gn, wout, hp, gf1, w1, w2, gfin,
                       TILE_ML_OUT).reshape(b, s, d)
    p_conv = tail.reshape(b, s // tm1, halo_p, inner)[:, -1, halo_p - conv_buf:][None]

    halo_c = jnp.pad(_time_major(state_ml_conv[0]), ((ns, 0), (0, 0)))
    q1, k1, v1, gc, gr, sg, gsk, tail = _ml_in(hs, g1, wi, halo_c, *consts, tm=tms, g_seq=ns,
                                               tiles_per_seq=1, halo_steps=conv_buf + 1, chunk=ts)
    extra = ML_SAMPLE_PAD - ts
    pad16 = lambda t: jnp.pad(_seq_major(t, ns), ((0, 0), (0, extra), (0, 0)))
    gc = jnp.pad(_seq_major(gc, ns), ((0, 0), (0, extra), (0, 0)), mode="edge")
    gc = jnp.where((jnp.arange(ML_SAMPLE_PAD)[:, None] >= ts) & (jnp.arange(gc.shape[-1]) < ML_HEADS), NEG, gc)
    gr = jnp.transpose(gc[:, :, :2 * ML_HEADS], (0, 2, 1))
    m0 = jnp.broadcast_to(state_ml_m[0][:, :, None, None], (ns, ML_HEADS, 1, LANES))
    hcs, s_c, s_n, s_m = _ml_cell(pad16(q1), pad16(k1), pad16(v1), gc, gr,
                                  state_ml_C[0], state_ml_n[0][:, :, None, :], m0, chunk=ML_SAMPLE_PAD)
    y_sample = _seq_major(_ml_out(_time_major(hcs[:, :ts]), sg, gsk, gn, wout, hs, gf1, w1, w2, gfin, tms), ns)
    s_conv = _seq_major(tail[0], ns)[:, -conv_buf:][None]

    return (y_prompt, y_sample, p_ak, p_av, p_pool,
            p_c[None], p_n[:, :, 0][None], p_m[:, :, 0, 0][None], p_conv,
            s_ak, s_av, s_pool,
            s_c[None], s_n[:, :, 0][None], s_m[:, :, 0, 0][None], s_conv)
```
